```python
import math
import jax, jax.numpy as jnp
from jax import lax
import numpy as np

D_MODEL = 1024
BATCH = 2
SEQ = 8192
DEPTH = 4

D_MIX = D_MODEL
POOL_WINDOWS = (2, 4, 8, 16)
POOL_GROUPS = len(POOL_WINDOWS)
POOL_CH = 64
POOL_WIDTH = POOL_GROUPS * POOL_CH
MAX_WIN = max(POOL_WINDOWS)
HGRN_WIDTH = D_MIX - POOL_WIDTH
HGRN_HEAD_DIM = 128
HGRN_HEADS = HGRN_WIDTH // HGRN_HEAD_DIM
CHUNK = 64
D_FF = 4 * D_MODEL
RMS_EPS = 1e-5
IN_WIDTH = POOL_WIDTH + 4 * HGRN_WIDTH

kernel_name = "hymba_pool_hgrn2_hybrid"


def rms_norm(x, g, eps=RMS_EPS):
    xf = x.astype(jnp.float32)
    y = xf * lax.rsqrt(jnp.mean(xf * xf, axis=-1, keepdims=True) + eps)
    return (y * g.astype(jnp.float32)).astype(x.dtype)


def pool_mixer(u, w, scale):
    B, S, _ = u.shape
    uf = u.astype(jnp.float32)
    c = jnp.cumsum(uf, axis=1)
    cp = jnp.pad(c, ((0, 0), (MAX_WIN, 0), (0, 0)))
    t = jnp.arange(S)
    groups = []
    for gi, win in enumerate(POOL_WINDOWS):
        lo, hi = gi * POOL_CH, (gi + 1) * POOL_CH
        win_sum = c[:, :, lo:hi] - cp[:, MAX_WIN - win:MAX_WIN - win + S, lo:hi]
        cnt = jnp.minimum(t + 1, win).astype(jnp.float32)[None, :, None]
        groups.append(win_sum / cnt - uf[:, :, lo:hi])
    p = jnp.stack(groups, axis=2)
    y = jnp.einsum('bsgc,gcd->bsgd', p, w.astype(jnp.float32)).reshape(B, S, POOL_WIDTH)
    return (y * scale.astype(jnp.float32)).astype(u.dtype)


def _to_chunks(t, B, S):
    return t.reshape(B, S // CHUNK, CHUNK, HGRN_HEADS, HGRN_HEAD_DIM).transpose(1, 0, 3, 2, 4)


def hgrn2_mixer(q_raw, f_raw, i_raw, g_raw, lb, norm_g):
    B, S, _ = q_raw.shape
    z = f_raw.astype(jnp.float32)
    lbf = lb.astype(jnp.float32)
    log_f = jnp.logaddexp(jnp.log(lbf), jnp.log1p(-lbf) + jax.nn.log_sigmoid(z))
    k = (1.0 - lbf) * jax.nn.sigmoid(-z)
    q = jax.nn.silu(q_raw.astype(jnp.float32))
    v = i_raw.astype(jnp.float32)
    qc, kc, vc, fc = (_to_chunks(a, B, S) for a in (q, k, v, log_f))
    causal = jnp.tril(jnp.ones((CHUNK, CHUNK), dtype=bool))[None, None, :, :, None]

    def chunk_step(state, inp):
        qb, kb, vb, lfb = inp
        b = jnp.cumsum(lfb, axis=2)
        diff = jnp.where(causal, b[:, :, :, None, :] - b[:, :, None, :, :], -jnp.inf)
        scores = jnp.einsum('bhtk,bhsk,bhtsk->bhts', qb, kb, jnp.exp(diff))
        o = (jnp.einsum('bhts,bhsv->bhtv', scores, vb)
             + jnp.einsum('bhtk,bhkv->bhtv', qb * jnp.exp(b), state))
        b_last = b[:, :, -1:, :]
        new_state = (jnp.exp(b_last[:, :, 0, :])[..., None] * state
                     + jnp.einsum('bhsk,bhsv->bhkv', kb * jnp.exp(b_last - b), vb))
        return new_state, o

    s0 = jnp.zeros((B, HGRN_HEADS, HGRN_HEAD_DIM, HGRN_HEAD_DIM), jnp.float32)
    _, o = lax.scan(chunk_step, s0, (qc, kc, vc, fc))
    o = o.transpose(1, 0, 3, 2, 4).reshape(B, S, HGRN_HEADS, HGRN_HEAD_DIM)
    o = o * lax.rsqrt(jnp.mean(o * o, axis=-1, keepdims=True) + RMS_EPS)
    o = o.reshape(B, S, HGRN_WIDTH) * norm_g.astype(jnp.float32)
    o = o * jax.nn.silu(g_raw.astype(jnp.float32))
    return o.astype(q_raw.dtype)


def setup_inputs(seed: int = 0) -> dict:
    key = jax.random.key(seed)
    ks = jax.random.split(key, 12)
    f32 = jnp.float32
    x = jax.random.normal(ks[0], (BATCH, SEQ, D_MODEL), f32)
    norm_mix_g = 1.0 + 0.05 * jax.random.normal(ks[1], (DEPTH, D_MODEL), f32)
    w_in = jax.random.normal(ks[2], (DEPTH, D_MODEL, IN_WIDTH), f32) * D_MODEL ** -0.5
    pool_w = jax.random.normal(ks[3], (DEPTH, POOL_GROUPS, POOL_CH, POOL_CH), f32) * POOL_CH ** -0.5
    pool_scale = 1.0 + 0.1 * jax.random.normal(ks[4], (DEPTH, POOL_WIDTH), f32)
    hgrn_lb_logits = 0.5 * jax.random.normal(ks[5], (DEPTH, HGRN_WIDTH), f32)
    hgrn_norm_g = 1.0 + 0.05 * jax.random.normal(ks[6], (DEPTH, HGRN_WIDTH), f32)
    w_out = jax.random.normal(ks[7], (DEPTH, D_MIX, D_MODEL), f32) * D_MIX ** -0.5
    norm_mlp_g = 1.0 + 0.05 * jax.random.normal(ks[8], (DEPTH, D_MODEL), f32)
    w_up = jax.random.normal(ks[9], (DEPTH, D_MODEL, D_FF), f32) * D_MODEL ** -0.5
    w_down = jax.random.normal(ks[10], (DEPTH, D_FF, D_MODEL), f32) * D_FF ** -0.5
    final_norm_g = 1.0 + 0.05 * jax.random.normal(ks[11], (D_MODEL,), f32)
    return {"x": x, "norm_mix_g": norm_mix_g, "w_in": w_in, "pool_w": pool_w,
            "pool_scale": pool_scale, "hgrn_lb_logits": hgrn_lb_logits,
            "hgrn_norm_g": hgrn_norm_g, "w_out": w_out, "norm_mlp_g": norm_mlp_g,
            "w_up": w_up, "w_down": w_down, "final_norm_g": final_norm_g}


def reference(x, norm_mix_g, w_in, pool_w, pool_scale, hgrn_lb_logits, hgrn_norm_g,
              w_out, norm_mlp_g, w_up, w_down, final_norm_g):
    lb_cum = jnp.cumsum(jax.nn.softmax(hgrn_lb_logits.astype(jnp.float32), axis=0), axis=0)
    lower_bounds = lb_cum - lb_cum[0:1]
    splits = [POOL_WIDTH + j * HGRN_WIDTH for j in range(4)]
    for l in range(DEPTH):
        h = rms_norm(x, norm_mix_g[l])
        u = h @ w_in[l]
        u_pool, q_raw, f_raw, i_raw, g_raw = jnp.split(u, splits, axis=-1)
        y_pool = pool_mixer(u_pool, pool_w[l], pool_scale[l])
        y_hgrn = hgrn2_mixer(q_raw, f_raw, i_raw, g_raw, lower_bounds[l], hgrn_norm_g[l])
        x = x + jnp.concatenate([y_pool, y_hgrn], axis=-1) @ w_out[l]
        h2 = rms_norm(x, norm_mlp_g[l])
        x = x + jnp.square(jax.nn.relu(h2 @ w_up[l])) @ w_down[l]
    return rms_norm(x, final_norm_g)
```

```python
import functools

import jax
import jax.numpy as jnp
from jax import lax
from jax.experimental import pallas as pl
from jax.experimental.pallas import tpu as pltpu

D_MODEL = 1024
POOL_WINDOWS = (2, 4, 8, 16)
POOL_CH = 64
POOL_WIDTH = len(POOL_WINDOWS) * POOL_CH
MAX_WIN = max(POOL_WINDOWS)
HEAD_DIM = 128
HGRN_WIDTH = D_MODEL - POOL_WIDTH
HEADS = HGRN_WIDTH // HEAD_DIM
IN_WIDTH = POOL_WIDTH + 4 * HGRN_WIDTH
D_FF = 4 * D_MODEL
RMS_EPS = 1e-5

LANES = 128
SUBLANES = 8
CHUNK = 64
NBLK = CHUNK // SUBLANES
STACK_COLS = SUBLANES * (NBLK * (NBLK - 1) // 2)
STACK_PAD = 256
VMEM_LIMIT = 56 * 1024 * 1024

BF16 = jnp.bfloat16
F32 = jnp.float32


def _rms(x, g):
    ms = jnp.mean(x * x, axis=-1, keepdims=True)
    return x * lax.rsqrt(ms + RMS_EPS) * g


def _in_proj_kernel(x_ref, g_ref, w_ref, u_ref):
    h = _rms(x_ref[...], g_ref[...]).astype(BF16)
    u_ref[...] = jnp.dot(h, w_ref[...], preferred_element_type=F32)


def _in_proj(x2, g, w_bf, tm):
    t = x2.shape[0]
    return pl.pallas_call(
        _in_proj_kernel,
        grid=(t // tm,),
        in_specs=[
            pl.BlockSpec((tm, D_MODEL), lambda i: (i, 0)),
            pl.BlockSpec((1, D_MODEL), lambda i: (0, 0)),
            pl.BlockSpec((D_MODEL, IN_WIDTH), lambda i: (0, 0)),
        ],
        out_specs=pl.BlockSpec((tm, IN_WIDTH), lambda i: (i, 0)),
        out_shape=jax.ShapeDtypeStruct((t, IN_WIDTH), F32),
        compiler_params=pltpu.CompilerParams(
            dimension_semantics=("arbitrary",), vmem_limit_bytes=VMEM_LIMIT),
        name="in_proj",
    )(x2, g, w_bf)


def _pool_kernel(u_ref, w_ref, sc_ref, y_ref, buf_ref, *, ts):
    s_idx = pl.program_id(1)

    @pl.when(s_idx == 0)
    def _():
        buf_ref[0:MAX_WIN, :] = jnp.zeros((MAX_WIN, POOL_WIDTH), F32)

    @pl.when(s_idx != 0)
    def _():
        buf_ref[0:MAX_WIN, :] = buf_ref[ts:ts + MAX_WIN, :]

    buf_ref[MAX_WIN:MAX_WIN + ts, :] = u_ref[...]

    t_glob = s_idx * ts + lax.broadcasted_iota(jnp.int32, (ts, LANES), 0)
    lane = lax.broadcasted_iota(jnp.int32, (ts, LANES), 1)
    first_group = lane < POOL_CH

    def window_sum(col, lo, hi):
        acc = None
        for d in range(lo, hi):
            v = buf_ref[MAX_WIN - d:MAX_WIN - d + ts, col:col + LANES]
            acc = v if acc is None else acc + v
        return acc

    parts = []
    for half in range(2):
        col = half * LANES
        w_a, w_b = POOL_WINDOWS[2 * half], POOL_WINDOWS[2 * half + 1]
        sum_a = window_sum(col, 0, w_a)
        sum_b = sum_a + window_sum(col, w_a, w_b)
        cnt_a = jnp.minimum(t_glob + 1, w_a).astype(F32)
        cnt_b = jnp.minimum(t_glob + 1, w_b).astype(F32)
        mean = jnp.where(first_group, sum_a / cnt_a, sum_b / cnt_b)
        parts.append(mean - u_ref[:, col:col + LANES])
    p = jnp.concatenate(parts, axis=1).astype(BF16)
    y = jnp.dot(p, w_ref[...], preferred_element_type=F32) * sc_ref[...]
    y_ref[...] = y.astype(y_ref.dtype)


def _pool(u3, w_bd, scale, ts):
    b, s, _ = u3.shape
    return pl.pallas_call(
        functools.partial(_pool_kernel, ts=ts),
        grid=(b, s // ts),
        in_specs=[
            pl.BlockSpec((None, ts, POOL_WIDTH), lambda bi, si: (bi, si, 0)),
            pl.BlockSpec((POOL_WIDTH, POOL_WIDTH), lambda bi, si: (0, 0)),
            pl.BlockSpec((1, POOL_WIDTH), lambda bi, si: (0, 0)),
        ],
        out_specs=pl.BlockSpec((None, ts, POOL_WIDTH), lambda bi, si: (bi, si, 0)),
        out_shape=jax.ShapeDtypeStruct((b, s, POOL_WIDTH), BF16),
        scratch_shapes=[pltpu.VMEM((ts + MAX_WIN, POOL_WIDTH), F32)],
        compiler_params=pltpu.CompilerParams(
            dimension_semantics=("arbitrary", "arbitrary"), vmem_limit_bytes=VMEM_LIMIT),
        name="pool_mixer",
    )(u3, w_bd, scale)


def _bcast_row(x, r):
    return jnp.broadcast_to(x[r:r + 1, :], x.shape)


def _hgrn_chunk(qr, z, v, gr, lb, one_m_lb, log_one_m_lb, norm_g, st, mask_p):
    e = jnp.exp(-jnp.abs(z))
    d = 1.0 + e
    r = 1.0 / d
    er = e * r
    pos = z >= 0.0
    sig_p = jnp.where(pos, r, er)
    sig_n = jnp.where(pos, er, r)
    log_sig = jnp.minimum(z, 0.0) - jnp.log(d)
    f = lb + one_m_lb * sig_p
    logf = jnp.maximum(jnp.log(f), log_one_m_lb + log_sig)
    kk = one_m_lb * sig_n
    q = qr * (1.0 / (1.0 + jnp.exp(-qr)))

    row = lax.broadcasted_iota(jnp.int32, (SUBLANES, LANES), 0)
    blk = lambda a, j: a[SUBLANES * j:SUBLANES * (j + 1), :]

    cb = []
    for j in range(NBLK):
        c = blk(logf, j)
        for sh in (1, 2, 4):
            c = c + jnp.where(row >= sh, pltpu.roll(c, sh, 0), 0.0)
        cb.append(c)
    tot = [_bcast_row(c, SUBLANES - 1) for c in cb]
    carry = [jnp.zeros((SUBLANES, LANES), F32)]
    for j in range(NBLK):
        carry.append(carry[-1] + tot[j])

    qb = [blk(q, j) for j in range(NBLK)]
    kb = [blk(kk, j) for j in range(NBLK)]
    vb = [blk(v, j) for j in range(NBLK)]

    o_diag = []
    for j in range(NBLK):
        acc = jnp.zeros((SUBLANES, LANES), F32)
        for s in range(SUBLANES):
            dec = jnp.exp(jnp.minimum(cb[j] - _bcast_row(cb[j], s), 0.0))
            a = jnp.where(row >= s, qb[j] * _bcast_row(kb[j], s) * dec, 0.0)
            score = jnp.sum(a, axis=1, keepdims=True)
            acc = acc + score * _bcast_row(vb[j], s)
        o_diag.append(acc)

    q_blk = jnp.concatenate([qb[j] * jnp.exp(cb[j]) for j in range(NBLK)], axis=0)
    q_chunk = jnp.concatenate([qb[j] * jnp.exp(cb[j] + carry[j]) for j in range(NBLK)], axis=0)
    k_hat = [kb[j] * jnp.exp(tot[j] - cb[j]) for j in range(NBLK)]
    e_tot = [jnp.exp(tot[j]) for j in range(NBLK)]
    cur = {}
    k_stack, v_stack = [], []
    for i in range(1, NBLK + 1):
        for j in range(i - 1):
            cur[j] = cur[j] * e_tot[i - 1]
        cur[i - 1] = k_hat[i - 1]
        if i < NBLK:
            for j in range(i):
                k_stack.append(cur[j])
                v_stack.append(vb[j])
    k_chunk = jnp.concatenate([cur[j] for j in range(NBLK)], axis=0)
    pad = [jnp.zeros((STACK_PAD - STACK_COLS, LANES), F32)]
    k_stack = jnp.concatenate(k_stack + pad, axis=0).astype(BF16)
    v_stack = jnp.concatenate(v_stack + pad, axis=0).astype(BF16)

    nt = (((1,), (1,)), ((), ()))
    tn = (((0,), (0,)), ((), ()))
    s_off = lax.dot_general(q_blk.astype(BF16), k_stack, nt, preferred_element_type=F32)
    p_off = (s_off * mask_p).astype(BF16)
    o_off = jnp.dot(p_off, v_stack, preferred_element_type=F32)
    o_int = lax.dot_general(q_chunk.astype(BF16), st.astype(BF16), nt,
                            preferred_element_type=F32)
    kv = lax.dot_general(v.astype(BF16), k_chunk.astype(BF16), tn,
                         preferred_element_type=F32)
    st_new = st * jnp.exp(carry[NBLK][0:1, :]) + kv

    o = jnp.concatenate(o_diag, axis=0) + o_off + o_int
    o = _rms(o, norm_g)
    y = o * (gr * (1.0 / (1.0 + jnp.exp(-gr))))
    return y, st_new


def _hgrn_kernel(q_ref, z_ref, v_ref, g_ref, lbp_ref, ng_ref, mask_ref, y_ref, st_ref, *, ts):
    @pl.when(pl.program_id(2) == 0)
    def _():
        st_ref[...] = jnp.zeros((HEAD_DIM, HEAD_DIM), F32)

    lb = lbp_ref[0:1, :]
    one_m_lb = lbp_ref[1:2, :]
    log_one_m_lb = lbp_ref[2:3, :]
    norm_g = ng_ref[...]
    mask_p = mask_ref[...]

    def body(c, carry):
        r0 = pl.multiple_of(c * CHUNK, CHUNK)
        rows = pl.ds(r0, CHUNK)
        y, st_new = _hgrn_chunk(q_ref[rows, :], z_ref[rows, :], v_ref[rows, :], g_ref[rows, :],
                                lb, one_m_lb, log_one_m_lb, norm_g, st_ref[...], mask_p)
        st_ref[...] = st_new
        y_ref[rows, :] = y.astype(y_ref.dtype)
        return carry

    lax.fori_loop(0, ts // CHUNK, body, 0)


def _offdiag_mask():
    col_blk = []
    for i in range(1, NBLK):
        col_blk += [i] * (SUBLANES * i)
    col_blk += [-1] * (STACK_PAD - STACK_COLS)
    col_blk = jnp.asarray(col_blk, jnp.int32)[None, :]
    row_blk = (jnp.arange(CHUNK, dtype=jnp.int32) // SUBLANES)[:, None]
    return (row_blk == col_blk).astype(F32)


def _hgrn(u3, lb_params, norm_g, ts):
    b, s, _ = u3.shape
    pool_blocks = POOL_WIDTH // LANES

    def col_spec(section):
        off = pool_blocks + section * HEADS
        return pl.BlockSpec((None, ts, HEAD_DIM), lambda bi, hi, si: (bi, si, off + hi))

    return pl.pallas_call(
        functools.partial(_hgrn_kernel, ts=ts),
        grid=(b, HEADS, s // ts),
        in_specs=[
            col_spec(0), col_spec(1), col_spec(2), col_spec(3),
            pl.BlockSpec((None, 8, HEAD_DIM), lambda bi, hi, si: (hi, 0, 0)),
            pl.BlockSpec((None, 1, HEAD_DIM), lambda bi, hi, si: (hi, 0, 0)),
            pl.BlockSpec((CHUNK, STACK_PAD), lambda bi, hi, si: (0, 0)),
        ],
        out_specs=pl.BlockSpec((None, ts, HEAD_DIM), lambda bi, hi, si: (bi, si, hi)),
        out_shape=jax.ShapeDtypeStruct((b, s, HGRN_WIDTH), BF16),
        scratch_shapes=[pltpu.VMEM((HEAD_DIM, HEAD_DIM), F32)],
        compiler_params=pltpu.CompilerParams(
            dimension_semantics=("arbitrary", "arbitrary", "arbitrary"),
            vmem_limit_bytes=VMEM_LIMIT),
        name="hgrn_mixer",
    )(u3, u3, u3, u3, lb_params, norm_g, _offdiag_mask())


def _out_proj_kernel(x_ref, yp_ref, yh_ref, wp_ref, wh_ref, o_ref):
    acc = jnp.dot(yp_ref[...], wp_ref[...], preferred_element_type=F32)
    acc = acc + jnp.dot(yh_ref[...], wh_ref[...], preferred_element_type=F32)
    o_ref[...] = x_ref[...] + acc


def _out_proj(x2, yp2, yh2, wp_bf, wh_bf, tm):
    t = x2.shape[0]
    return pl.pallas_call(
        _out_proj_kernel,
        grid=(t // tm,),
        in_specs=[
            pl.BlockSpec((tm, D_MODEL), lambda i: (i, 0)),
            pl.BlockSpec((tm, POOL_WIDTH), lambda i: (i, 0)),
            pl.BlockSpec((tm, HGRN_WIDTH), lambda i: (i, 0)),
            pl.BlockSpec((POOL_WIDTH, D_MODEL), lambda i: (0, 0)),
            pl.BlockSpec((HGRN_WIDTH, D_MODEL), lambda i: (0, 0)),
        ],
        out_specs=pl.BlockSpec((tm, D_MODEL), lambda i: (i, 0)),
        out_shape=jax.ShapeDtypeStruct((t, D_MODEL), F32),
        compiler_params=pltpu.CompilerParams(
            dimension_semantics=("arbitrary",), vmem_limit_bytes=VMEM_LIMIT),
        name="out_proj",
    )(x2, yp2, yh2, wp_bf, wh_bf)


def _mlp_kernel(x_ref, g_ref, wu_ref, wd_ref, o_ref, *, ff_tile):
    x = x_ref[...]
    h = _rms(x, g_ref[...]).astype(BF16)
    acc = x
    for n in range(D_FF // ff_tile):
        cols = slice(n * ff_tile, (n + 1) * ff_tile)
        a = jnp.dot(h, wu_ref[:, cols], preferred_element_type=F32)
        a = jnp.square(jnp.maximum(a, 0.0)).astype(BF16)
        acc = acc + jnp.dot(a, wd_ref[cols, :], preferred_element_type=F32)
    o_ref[...] = acc


def _mlp(x2, g, wu_bf, wd_bf, tm, ff_tile=1024):
    t = x2.shape[0]
    return pl.pallas_call(
        functools.partial(_mlp_kernel, ff_tile=ff_tile),
        grid=(t // tm,),
        in_specs=[
            pl.BlockSpec((tm, D_MODEL), lambda i: (i, 0)),
            pl.BlockSpec((1, D_MODEL), lambda i: (0, 0)),
            pl.BlockSpec((D_MODEL, D_FF), lambda i: (0, 0)),
            pl.BlockSpec((D_FF, D_MODEL), lambda i: (0, 0)),
        ],
        out_specs=pl.BlockSpec((tm, D_MODEL), lambda i: (i, 0)),
        out_shape=jax.ShapeDtypeStruct((t, D_MODEL), F32),
        compiler_params=pltpu.CompilerParams(
            dimension_semantics=("arbitrary",), vmem_limit_bytes=VMEM_LIMIT),
        name="mlp",
    )(x2, g, wu_bf, wd_bf)


def _final_norm_kernel(x_ref, g_ref, o_ref):
    o_ref[...] = _rms(x_ref[...], g_ref[...])


def _final_norm(x2, g, tm):
    t = x2.shape[0]
    return pl.pallas_call(
        _final_norm_kernel,
        grid=(t // tm,),
        in_specs=[
            pl.BlockSpec((tm, D_MODEL), lambda i: (i, 0)),
            pl.BlockSpec((1, D_MODEL), lambda i: (0, 0)),
        ],
        out_specs=pl.BlockSpec((tm, D_MODEL), lambda i: (i, 0)),
        out_shape=jax.ShapeDtypeStruct((t, D_MODEL), F32),
        compiler_params=pltpu.CompilerParams(dimension_semantics=("arbitrary",)),
        name="final_norm",
    )(x2, g)


def _block_diag(w):
    g, c, _ = w.shape
    eye = jnp.eye(g, dtype=w.dtype)
    return (eye[:, None, :, None] * w[:, :, None, :]).reshape(g * c, g * c)


def kernel(x, norm_mix_g, w_in, pool_w, pool_scale, hgrn_lb_logits, hgrn_norm_g, w_out,
           norm_mlp_g, w_up, w_down, final_norm_g):
    b, s, d = x.shape
    depth = w_in.shape[0]
    t = b * s
    tm = 512
    ts_pool = 1024
    ts_hgrn = 1024

    lb_cum = jnp.cumsum(jax.nn.softmax(hgrn_lb_logits.astype(F32), axis=0), axis=0)
    lower = lb_cum - lb_cum[0:1]
    lbp = jnp.stack([lower, 1.0 - lower, jnp.log1p(-lower)], axis=1)
    lbp = jnp.pad(lbp, ((0, 0), (0, 5), (0, 0)))
    lbp = lbp.reshape(depth, 8, HEADS, HEAD_DIM).transpose(0, 2, 1, 3)

    x2 = x.reshape(t, d)
    for l in range(depth):
        u = _in_proj(x2, norm_mix_g[l][None, :], w_in[l].astype(BF16), tm)
        u3 = u.reshape(b, s, IN_WIDTH)
        y_pool = _pool(u3, _block_diag(pool_w[l]).astype(BF16), pool_scale[l][None, :], ts_pool)
        y_hgrn = _hgrn(u3, lbp[l], hgrn_norm_g[l].reshape(HEADS, 1, HEAD_DIM), ts_hgrn)
        wo = w_out[l].astype(BF16)
        x2 = _out_proj(x2, y_pool.reshape(t, POOL_WIDTH), y_hgrn.reshape(t, HGRN_WIDTH),
                       wo[:POOL_WIDTH], wo[POOL_WIDTH:], tm)
        x2 = _mlp(x2, norm_mlp_g[l][None, :], w_up[l].astype(BF16), w_down[l].astype(BF16), tm)
    out = _final_norm(x2, final_norm_g[None, :], tm)
    return out.reshape(b, s, d)
```

```python
import functools

import jax
import jax.numpy as jnp
from jax import lax
from jax.experimental import pallas as pl
from jax.experimental.pallas import tpu as pltpu

D_MODEL = 1024
POOL_WINDOWS = (2, 4, 8, 16)
POOL_CH = 64
POOL_WIDTH = len(POOL_WINDOWS) * POOL_CH
MAX_WIN = max(POOL_WINDOWS)
HEAD_DIM = 128
HGRN_WIDTH = D_MODEL - POOL_WIDTH
HEADS = HGRN_WIDTH // HEAD_DIM
IN_WIDTH = POOL_WIDTH + 4 * HGRN_WIDTH
D_FF = 4 * D_MODEL
RMS_EPS = 1e-5
LOG2_E = 1.4426950408889634

LANES = 128
SUBLANES = 8
CHUNK = 64
NBLK = CHUNK // SUBLANES
STACK_COLS = SUBLANES * (NBLK * (NBLK - 1) // 2)
STACK_PAD = 256
HEADS_PER_STEP = 6
POOL_COL_BLOCK = 4 * HGRN_WIDTH // POOL_WIDTH
VMEM_LIMIT = 56 * 1024 * 1024

BF16 = jnp.bfloat16
F32 = jnp.float32


def _rms(x, g):
    ms = jnp.mean(x * x, axis=-1, keepdims=True)
    return x * lax.rsqrt(ms + RMS_EPS) * g


def _in_proj_kernel(x_ref, g_ref, w_ref, u_ref):
    h = _rms(x_ref[...], g_ref[...]).astype(BF16)
    u_ref[...] = jnp.dot(h, w_ref[...], preferred_element_type=F32)


def _in_proj(x2, g, w_bf, tm):
    t = x2.shape[0]
    return pl.pallas_call(
        _in_proj_kernel,
        grid=(t // tm,),
        in_specs=[
            pl.BlockSpec((tm, D_MODEL), lambda i: (i, 0)),
            pl.BlockSpec((1, D_MODEL), lambda i: (0, 0)),
            pl.BlockSpec((D_MODEL, IN_WIDTH), lambda i: (0, 0)),
        ],
        out_specs=pl.BlockSpec((tm, IN_WIDTH), lambda i: (i, 0)),
        out_shape=jax.ShapeDtypeStruct((t, IN_WIDTH), F32),
        compiler_params=pltpu.CompilerParams(
            dimension_semantics=("arbitrary",), vmem_limit_bytes=VMEM_LIMIT),
        name="in_proj",
    )(x2, g, w_bf)


def _pool_kernel(u_ref, w_ref, sc_ref, y_ref, buf_ref, *, ts):
    s_idx = pl.program_id(1)

    @pl.when(s_idx == 0)
    def _():
        buf_ref[0:MAX_WIN, :] = jnp.zeros((MAX_WIN, POOL_WIDTH), F32)

    @pl.when(s_idx != 0)
    def _():
        buf_ref[0:MAX_WIN, :] = buf_ref[ts:ts + MAX_WIN, :]

    buf_ref[MAX_WIN:MAX_WIN + ts, :] = u_ref[...]

    t_glob = s_idx * ts + lax.broadcasted_iota(jnp.int32, (ts, LANES), 0)
    lane = lax.broadcasted_iota(jnp.int32, (ts, LANES), 1)
    first_group = lane < POOL_CH

    def window_sum(col, lo, hi):
        acc = None
        for d in range(lo, hi):
            v = buf_ref[MAX_WIN - d:MAX_WIN - d + ts, col:col + LANES]
            acc = v if acc is None else acc + v
        return acc

    parts = []
    for half in range(2):
        col = half * LANES
        w_a, w_b = POOL_WINDOWS[2 * half], POOL_WINDOWS[2 * half + 1]
        sum_a = window_sum(col, 0, w_a)
        sum_b = sum_a + window_sum(col, w_a, w_b)
        cnt_a = jnp.minimum(t_glob + 1, w_a).astype(F32)
        cnt_b = jnp.minimum(t_glob + 1, w_b).astype(F32)
        mean = jnp.where(first_group, sum_a / cnt_a, sum_b / cnt_b)
        parts.append(mean - u_ref[:, col:col + LANES])
    p = jnp.concatenate(parts, axis=1).astype(BF16)
    y = jnp.dot(p, w_ref[...], preferred_element_type=F32) * sc_ref[...]
    y_ref[...] = y.astype(y_ref.dtype)


def _pool(u3, w_bd, scale, ts):
    b, s, _ = u3.shape
    return pl.pallas_call(
        functools.partial(_pool_kernel, ts=ts),
        grid=(b, s // ts),
        in_specs=[
            pl.BlockSpec((None, ts, POOL_WIDTH), lambda bi, si: (bi, si, POOL_COL_BLOCK)),
            pl.BlockSpec((POOL_WIDTH, POOL_WIDTH), lambda bi, si: (0, 0)),
            pl.BlockSpec((1, POOL_WIDTH), lambda bi, si: (0, 0)),
        ],
        out_specs=pl.BlockSpec((None, ts, POOL_WIDTH), lambda bi, si: (bi, si, 0)),
        out_shape=jax.ShapeDtypeStruct((b, s, POOL_WIDTH), BF16),
        scratch_shapes=[pltpu.VMEM((ts + MAX_WIN, POOL_WIDTH), F32)],
        compiler_params=pltpu.CompilerParams(
            dimension_semantics=("arbitrary", "arbitrary"), vmem_limit_bytes=VMEM_LIMIT),
        name="pool_mixer",
    )(u3, w_bd, scale)


def _bcast_row(x, r):
    return jnp.broadcast_to(x[r:r + 1, :], x.shape)


def _hgrn_scores(qr, z, v, lb, one_m_lb, log_one_m_lb, st, mask_d):
    e = jnp.exp(-jnp.abs(z))
    d = 1.0 + e
    r = 1.0 / d
    er = e * r
    pos = z >= 0.0
    sig_p = jnp.where(pos, r, er)
    sig_n = jnp.where(pos, er, r)
    log_sig = jnp.minimum(z, 0.0) - jnp.log(d)
    f = lb + one_m_lb * sig_p
    logf = jnp.maximum(jnp.log(f), log_one_m_lb + log_sig)
    kk = one_m_lb * sig_n
    q = qr * (1.0 / (1.0 + jnp.exp(-qr)))

    row = lax.broadcasted_iota(jnp.int32, (SUBLANES, LANES), 0)
    lane = lax.broadcasted_iota(jnp.int32, (SUBLANES, LANES), 1)
    blk = lambda a, j: a[SUBLANES * j:SUBLANES * (j + 1), :]

    logf2 = logf * LOG2_E
    cb = []
    for j in range(NBLK):
        c = blk(logf2, j)
        for sh in (1, 2, 4):
            c = c + jnp.where(row >= sh, pltpu.roll(c, sh, 0), 0.0)
        cb.append(c)
    tot = [_bcast_row(c, SUBLANES - 1) for c in cb]
    carry = [jnp.zeros((SUBLANES, LANES), F32)]
    for j in range(NBLK):
        carry.append(carry[-1] + tot[j])

    qb = [blk(q, j) for j in range(NBLK)]
    kb = [blk(kk, j) for j in range(NBLK)]
    vb = [blk(v, j) for j in range(NBLK)]

    p_diag = []
    for j in range(NBLK):
        acc = jnp.zeros((SUBLANES, LANES), F32)
        for s in range(SUBLANES):
            dec = jnp.exp2(cb[j] - _bcast_row(cb[j], s))
            a = qb[j] * _bcast_row(kb[j], s) * dec
            score = jnp.sum(a, axis=1, keepdims=True)
            acc = jnp.where(lane == SUBLANES * j + s, score, acc)
        p_diag.append(acc)
    p_diag = jnp.where(mask_d > 0.0, jnp.concatenate(p_diag, axis=0), 0.0)
    p_diag = p_diag[:, :CHUNK].astype(BF16)

    q_blk = jnp.concatenate([qb[j] * jnp.exp2(cb[j]) for j in range(NBLK)], axis=0)
    q_chunk = jnp.concatenate([qb[j] * jnp.exp2(cb[j] + carry[j]) for j in range(NBLK)], axis=0)
    k_hat = [kb[j] * jnp.exp2(tot[j] - cb[j]) for j in range(NBLK)]
    e_tot = [jnp.exp2(tot[j]) for j in range(NBLK)]
    cur = {}
    k_stack, v_stack = [], []
    for i in range(1, NBLK + 1):
        for j in range(i - 1):
            cur[j] = cur[j] * e_tot[i - 1]
        cur[i - 1] = k_hat[i - 1]
        if i < NBLK:
            for j in range(i):
                k_stack.append(cur[j])
                v_stack.append(vb[j])
    k_chunk = jnp.concatenate([cur[j] for j in range(NBLK)], axis=0)
    pad = [jnp.zeros((STACK_PAD - STACK_COLS, LANES), F32)]
    k_stack = jnp.concatenate(k_stack + pad, axis=0).astype(BF16)
    v_stack = jnp.concatenate(v_stack + pad, axis=0).astype(BF16)
    v_bf = v.astype(BF16)

    nt = (((1,), (1,)), ((), ()))
    tn = (((0,), (0,)), ((), ()))
    s_off = lax.dot_general(q_blk.astype(BF16), k_stack, nt, preferred_element_type=F32)
    o_int = lax.dot_general(q_chunk.astype(BF16), st.astype(BF16), nt,
                            preferred_element_type=F32)
    kv = lax.dot_general(v_bf, k_chunk.astype(BF16), tn, preferred_element_type=F32)
    st_decay = jnp.exp2(carry[NBLK][0:1, :])
    return dict(s_off=s_off, o_int=o_int, kv=kv, st_decay=st_decay, p_diag=p_diag,
                v_stack=v_stack, v_bf=v_bf)


def _hgrn_mix(a, st, mask_p):
    p_off = (a["s_off"] * mask_p).astype(BF16)
    o = jnp.dot(p_off, a["v_stack"], preferred_element_type=F32)
    o = o + jnp.dot(a["p_diag"], a["v_bf"], preferred_element_type=F32)
    return o + a["o_int"], st * a["st_decay"] + a["kv"]


def _hgrn_out(o, gr, norm_g):
    return _rms(o, norm_g) * (gr * (1.0 / (1.0 + jnp.exp(-gr))))


def _hgrn_kernel(q_ref, z_ref, v_ref, g_ref, lbp_ref, ng_ref, mask_p_ref, mask_d_ref,
                 y_ref, st_ref, *, ts, hp):
    @pl.when(pl.program_id(2) == 0)
    def _():
        st_ref[...] = jnp.zeros((hp, HEAD_DIM, HEAD_DIM), F32)

    mask_p = mask_p_ref[...]
    mask_d = mask_d_ref[...]
    head_cols = [slice(h * HEAD_DIM, (h + 1) * HEAD_DIM) for h in range(hp)]

    def body(c, carry):
        r0 = pl.multiple_of(c * CHUNK, CHUNK)
        rows = pl.ds(r0, CHUNK)
        stage1 = [
            _hgrn_scores(q_ref[rows, cols], z_ref[rows, cols], v_ref[rows, cols],
                         lbp_ref[h, 0:1, :], lbp_ref[h, 1:2, :], lbp_ref[h, 2:3, :],
                         st_ref[h], mask_d)
            for h, cols in enumerate(head_cols)]
        stage2 = []
        for h in range(hp):
            o, st_new = _hgrn_mix(stage1[h], st_ref[h], mask_p)
            st_ref[h] = st_new
            stage2.append(o)
        for h, cols in enumerate(head_cols):
            y = _hgrn_out(stage2[h], g_ref[rows, cols], ng_ref[h])
            y_ref[rows, cols] = y.astype(y_ref.dtype)
        return carry

    lax.fori_loop(0, ts // CHUNK, body, 0)


def _score_masks():
    col_blk = []
    for i in range(1, NBLK):
        col_blk += [i] * (SUBLANES * i)
    col_blk += [-1] * (STACK_PAD - STACK_COLS)
    col_blk = jnp.asarray(col_blk, jnp.int32)[None, :]
    t = jnp.arange(CHUNK, dtype=jnp.int32)[:, None]
    mask_p = (t // SUBLANES == col_blk).astype(F32)
    c = jnp.arange(LANES, dtype=jnp.int32)[None, :]
    mask_d = ((c // SUBLANES == t // SUBLANES) & (c % SUBLANES <= t % SUBLANES)).astype(F32)
    return mask_p, mask_d


def _hgrn(u3, lb_params, norm_g, ts, hp):
    b, s, _ = u3.shape
    groups = HEADS // hp

    def col_spec(section):
        off = section * groups
        return pl.BlockSpec((None, ts, hp * HEAD_DIM), lambda bi, hi, si: (bi, si, off + hi))

    mask_p, mask_d = _score_masks()
    return pl.pallas_call(
        functools.partial(_hgrn_kernel, ts=ts, hp=hp),
        grid=(b, groups, s // ts),
        in_specs=[
            col_spec(0), col_spec(1), col_spec(2), col_spec(3),
            pl.BlockSpec((hp, 8, HEAD_DIM), lambda bi, hi, si: (hi, 0, 0)),
            pl.BlockSpec((hp, 1, HEAD_DIM), lambda bi, hi, si: (hi, 0, 0)),
            pl.BlockSpec((CHUNK, STACK_PAD), lambda bi, hi, si: (0, 0)),
            pl.BlockSpec((CHUNK, LANES), lambda bi, hi, si: (0, 0)),
        ],
        out_specs=pl.BlockSpec((None, ts, hp * HEAD_DIM), lambda bi, hi, si: (bi, si, hi)),
        out_shape=jax.ShapeDtypeStruct((b, s, HGRN_WIDTH), BF16),
        scratch_shapes=[pltpu.VMEM((hp, HEAD_DIM, HEAD_DIM), F32)],
        compiler_params=pltpu.CompilerParams(
            dimension_semantics=("arbitrary", "arbitrary", "arbitrary"),
            vmem_limit_bytes=VMEM_LIMIT),
        name="hgrn_mixer",
    )(u3, u3, u3, u3, lb_params, norm_g, mask_p, mask_d)


def _out_proj_kernel(x_ref, yp_ref, yh_ref, wp_ref, wh_ref, o_ref):
    acc = jnp.dot(yp_ref[...], wp_ref[...], preferred_element_type=F32)
    acc = acc + jnp.dot(yh_ref[...], wh_ref[...], preferred_element_type=F32)
    o_ref[...] = x_ref[...] + acc


def _out_proj(x2, yp2, yh2, wp_bf, wh_bf, tm):
    t = x2.shape[0]
    return pl.pallas_call(
        _out_proj_kernel,
        grid=(t // tm,),
        in_specs=[
            pl.BlockSpec((tm, D_MODEL), lambda i: (i, 0)),
            pl.BlockSpec((tm, POOL_WIDTH), lambda i: (i, 0)),
            pl.BlockSpec((tm, HGRN_WIDTH), lambda i: (i, 0)),
            pl.BlockSpec((POOL_WIDTH, D_MODEL), lambda i: (0, 0)),
            pl.BlockSpec((HGRN_WIDTH, D_MODEL), lambda i: (0, 0)),
        ],
        out_specs=pl.BlockSpec((tm, D_MODEL), lambda i: (i, 0)),
        out_shape=jax.ShapeDtypeStruct((t, D_MODEL), F32),
        compiler_params=pltpu.CompilerParams(
            dimension_semantics=("arbitrary",), vmem_limit_bytes=VMEM_LIMIT),
        name="out_proj",
    )(x2, yp2, yh2, wp_bf, wh_bf)


def _mlp_kernel(x_ref, g_ref, wu_ref, wd_ref, o_ref, *, ff_tile):
    x = x_ref[...]
    h = _rms(x, g_ref[...]).astype(BF16)
    acc = x
    for n in range(D_FF // ff_tile):
        cols = slice(n * ff_tile, (n + 1) * ff_tile)
        a = jnp.dot(h, wu_ref[:, cols], preferred_element_type=F32)
        a = jnp.square(jnp.maximum(a, 0.0)).astype(BF16)
        acc = acc + jnp.dot(a, wd_ref[cols, :], preferred_element_type=F32)
    o_ref[...] = acc


def _mlp(x2, g, wu_bf, wd_bf, tm, ff_tile=1024):
    t = x2.shape[0]
    return pl.pallas_call(
        functools.partial(_mlp_kernel, ff_tile=ff_tile),
        grid=(t // tm,),
        in_specs=[
            pl.BlockSpec((tm, D_MODEL), lambda i: (i, 0)),
            pl.BlockSpec((1, D_MODEL), lambda i: (0, 0)),
            pl.BlockSpec((D_MODEL, D_FF), lambda i: (0, 0)),
            pl.BlockSpec((D_FF, D_MODEL), lambda i: (0, 0)),
        ],
        out_specs=pl.BlockSpec((tm, D_MODEL), lambda i: (i, 0)),
        out_shape=jax.ShapeDtypeStruct((t, D_MODEL), F32),
        compiler_params=pltpu.CompilerParams(
            dimension_semantics=("arbitrary",), vmem_limit_bytes=VMEM_LIMIT),
        name="mlp",
    )(x2, g, wu_bf, wd_bf)


def _final_norm_kernel(x_ref, g_ref, o_ref):
    o_ref[...] = _rms(x_ref[...], g_ref[...])


def _final_norm(x2, g, tm):
    t = x2.shape[0]
    return pl.pallas_call(
        _final_norm_kernel,
        grid=(t // tm,),
        in_specs=[
            pl.BlockSpec((tm, D_MODEL), lambda i: (i, 0)),
            pl.BlockSpec((1, D_MODEL), lambda i: (0, 0)),
        ],
        out_specs=pl.BlockSpec((tm, D_MODEL), lambda i: (i, 0)),
        out_shape=jax.ShapeDtypeStruct((t, D_MODEL), F32),
        compiler_params=pltpu.CompilerParams(dimension_semantics=("arbitrary",)),
        name="final_norm",
    )(x2, g)


def _block_diag(w):
    g, c, _ = w.shape
    eye = jnp.eye(g, dtype=w.dtype)
    return (eye[:, None, :, None] * w[:, :, None, :]).reshape(g * c, g * c)


def kernel(x, norm_mix_g, w_in, pool_w, pool_scale, hgrn_lb_logits, hgrn_norm_g, w_out,
           norm_mlp_g, w_up, w_down, final_norm_g):
    b, s, d = x.shape
    depth = w_in.shape[0]
    t = b * s
    tm = 512
    ts_pool = 1024
    ts_hgrn = 512

    lb_cum = jnp.cumsum(jax.nn.softmax(hgrn_lb_logits.astype(F32), axis=0), axis=0)
    lower = lb_cum - lb_cum[0:1]
    lbp = jnp.stack([lower, 1.0 - lower, jnp.log1p(-lower)], axis=1)
    lbp = jnp.pad(lbp, ((0, 0), (0, 5), (0, 0)))
    lbp = lbp.reshape(depth, 8, HEADS, HEAD_DIM).transpose(0, 2, 1, 3)

    x2 = x.reshape(t, d)
    for l in range(depth):
        w_in_l = jnp.concatenate([w_in[l][:, POOL_WIDTH:], w_in[l][:, :POOL_WIDTH]], axis=1)
        u = _in_proj(x2, norm_mix_g[l][None, :], w_in_l.astype(BF16), tm)
        u3 = u.reshape(b, s, IN_WIDTH)
        y_pool = _pool(u3, _block_diag(pool_w[l]).astype(BF16), pool_scale[l][None, :], ts_pool)
        y_hgrn = _hgrn(u3, lbp[l], hgrn_norm_g[l].reshape(HEADS, 1, HEAD_DIM), ts_hgrn,
                       HEADS_PER_STEP)
        wo = w_out[l].astype(BF16)
        x2 = _out_proj(x2, y_pool.reshape(t, POOL_WIDTH), y_hgrn.reshape(t, HGRN_WIDTH),
                       wo[:POOL_WIDTH], wo[POOL_WIDTH:], tm)
        x2 = _mlp(x2, norm_mlp_g[l][None, :], w_up[l].astype(BF16), w_down[l].astype(BF16), tm)
    out = _final_norm(x2, final_norm_g[None, :], tm)
    return out.reshape(b, s, d)
```

```python
import functools

import jax
import jax.numpy as jnp
from jax import lax
from jax.experimental import pallas as pl
from jax.experimental.pallas import tpu as pltpu

D_MODEL = 1024
POOL_WINDOWS = (2, 4, 8, 16)
POOL_CH = 64
POOL_WIDTH = len(POOL_WINDOWS) * POOL_CH
MAX_WIN = max(POOL_WINDOWS)
HEAD_DIM = 128
HGRN_WIDTH = D_MODEL - POOL_WIDTH
HEADS = HGRN_WIDTH // HEAD_DIM
IN_WIDTH = POOL_WIDTH + 4 * HGRN_WIDTH
D_FF = 4 * D_MODEL
RMS_EPS = 1e-5
LOG2_E = 1.4426950408889634

LANES = 128
SUBLANES = 8
CHUNK = 64
NBLK = CHUNK // SUBLANES
BF16_TILE_BLOCKS = 2
STACK_ORDER = tuple(range(1, NBLK))
SUBTILE = 2 * CHUNK
PROJ_PIECE = 256
Q_COL, F_COL, I_COL, G_COL, POOL_COL = (j * HGRN_WIDTH for j in range(5))
VMEM_LIMIT = 56 * 1024 * 1024

BF16 = jnp.bfloat16
F32 = jnp.float32


def _group_blocks(i):
    return -(-i // BF16_TILE_BLOCKS) * BF16_TILE_BLOCKS


STACK_PAD = SUBLANES * sum(_group_blocks(i) for i in STACK_ORDER)


def _rms(x, g):
    ms = jnp.mean(x * x, axis=-1, keepdims=True)
    return x * lax.rsqrt(ms + RMS_EPS) * g


def _neg_abs(x):
    bits = lax.bitcast_convert_type(x, jnp.uint32) | jnp.uint32(0x80000000)
    return lax.bitcast_convert_type(bits, F32)


def _silu(x):
    h = 0.5 * x
    return h + h * jnp.tanh(h)


def _bcast_row(x, r):
    return jnp.broadcast_to(x[r:r + 1, :], x.shape)


def _hgrn_scores(qr, z, v, lb, one_m_lb, log_one_m_lb, st, mask_d, lk_ref):
    e = jnp.exp(_neg_abs(z))
    d = 1.0 + e
    r = 1.0 / d
    er = e * r
    pos = z >= 0.0
    sig_p = jnp.where(pos, r, er)
    sig_n = jnp.where(pos, er, r)
    log_sig = jnp.minimum(z, 0.0) - jnp.log(d)
    log_gate = log_one_m_lb + log_sig
    f = lb + one_m_lb * sig_p
    logf2 = jnp.maximum(jnp.log(f), log_gate) * LOG2_E
    kk = one_m_lb * sig_n
    log2_kk = (log_gate - z) * LOG2_E
    q = _silu(qr)

    row = lax.broadcasted_iota(jnp.int32, (SUBLANES, LANES), 0)
    lane = lax.broadcasted_iota(jnp.int32, (SUBLANES, LANES), 1)
    blk = lambda a, j: a[SUBLANES * j:SUBLANES * (j + 1), :]

    cb = []
    for j in range(NBLK):
        c = blk(logf2, j)
        for sh in (1, 2, 4):
            c = c + jnp.where(row >= sh, pltpu.roll(c, sh, 0), 0.0)
        cb.append(c)
    tot = [_bcast_row(c, SUBLANES - 1) for c in cb]
    carry = [jnp.zeros((SUBLANES, LANES), F32)]
    for j in range(NBLK):
        carry.append(carry[-1] + tot[j])

    qb = [blk(q, j) for j in range(NBLK)]
    kb = [blk(kk, j) for j in range(NBLK)]

    for j in range(NBLK):
        lk_ref[SUBLANES * j:SUBLANES * (j + 1), :] = blk(log2_kk, j) - cb[j]
    p_diag = []
    for j in range(NBLK):
        acc = jnp.zeros((SUBLANES, LANES), F32)
        for s in range(SUBLANES):
            lk_s = lk_ref[SUBLANES * j + s:SUBLANES * j + s + 1, :]
            a = qb[j] * jnp.exp2(cb[j] + jnp.broadcast_to(lk_s, (SUBLANES, LANES)))
            score = jnp.sum(a, axis=1, keepdims=True)
            acc = jnp.where(lane == SUBLANES * j + s, score, acc)
        p_diag.append(acc)
    p_diag = jnp.where(mask_d > 0.0, jnp.concatenate(p_diag, axis=0), 0.0)
    p_diag = p_diag[:, :CHUNK].astype(BF16)

    q_blk = jnp.concatenate([qb[j] * jnp.exp2(cb[j]) for j in range(NBLK)], axis=0)
    q_chunk = jnp.concatenate([qb[j] * jnp.exp2(cb[j] + carry[j]) for j in range(NBLK)], axis=0)
    k_hat = [kb[j] * jnp.exp2(tot[j] - cb[j]) for j in range(NBLK)]
    e_tot = [jnp.exp2(tot[j]) for j in range(NBLK)]
    cur = {}
    k_groups = {}
    for i in range(1, NBLK + 1):
        for j in range(i - 1):
            cur[j] = cur[j] * e_tot[i - 1]
        cur[i - 1] = k_hat[i - 1]
        if i < NBLK:
            k_groups[i] = [cur[j] for j in range(i)] + [k_hat[j] for j in range(i, _group_blocks(i))]
    k_chunk = jnp.concatenate([cur[j] for j in range(NBLK)], axis=0)
    v_bf = v.astype(BF16)
    k_stack = jnp.concatenate([p for i in STACK_ORDER for p in k_groups[i]], axis=0).astype(BF16)

    nt = (((1,), (1,)), ((), ()))
    tn = (((0,), (0,)), ((), ()))
    s_off = lax.dot_general(q_blk.astype(BF16), k_stack, nt, preferred_element_type=F32)
    o_int = lax.dot_general(q_chunk.astype(BF16), st.astype(BF16), nt,
                            preferred_element_type=F32)
    kv = lax.dot_general(v_bf, k_chunk.astype(BF16), tn, preferred_element_type=F32)
    st_new = st * jnp.exp2(carry[NBLK][0:1, :]) + kv
    return dict(s_off=s_off, o_int=o_int, p_diag=p_diag, v_bf=v_bf), st_new


def _hgrn_mix(a, mask_p):
    v_bf = a["v_bf"]
    v_stack = jnp.concatenate([v_bf[:SUBLANES * _group_blocks(i), :] for i in STACK_ORDER], axis=0)
    p_off = (a["s_off"] * mask_p).astype(BF16)
    o = jnp.dot(p_off, v_stack, preferred_element_type=F32)
    o = o + jnp.dot(a["p_diag"], v_bf, preferred_element_type=F32)
    return o + a["o_int"]


def _hgrn_out(o, gr, norm_g):
    return _rms(o, norm_g) * _silu(gr)


def _score_masks():
    col_blk = []
    for i in STACK_ORDER:
        col_blk += [i] * (SUBLANES * i) + [-1] * (SUBLANES * (_group_blocks(i) - i))
    col_blk = jnp.asarray(col_blk, jnp.int32)[None, :]
    t = jnp.arange(CHUNK, dtype=jnp.int32)[:, None]
    mask_p = (t // SUBLANES == col_blk).astype(F32)
    c = jnp.arange(LANES, dtype=jnp.int32)[None, :]
    mask_d = ((c // SUBLANES == t // SUBLANES) & (c % SUBLANES <= t % SUBLANES)).astype(F32)
    return mask_p, mask_d


def _pool_features(pbuf_ref, t0):
    t_glob = t0 + lax.broadcasted_iota(jnp.int32, (SUBTILE, LANES), 0)
    lane = lax.broadcasted_iota(jnp.int32, (SUBTILE, LANES), 1)
    first_group = lane < POOL_CH

    def window_sum(col, lo, hi):
        acc = None
        for d in range(lo, hi):
            v = pbuf_ref[MAX_WIN - d:MAX_WIN - d + SUBTILE, col:col + LANES]
            acc = v if acc is None else acc + v
        return acc

    parts = []
    for half in range(POOL_WIDTH // LANES):
        col = half * LANES
        w_a, w_b = POOL_WINDOWS[2 * half], POOL_WINDOWS[2 * half + 1]
        sum_a = window_sum(col, 0, w_a)
        sum_b = sum_a + window_sum(col, w_a, w_b)
        cnt_a = jnp.minimum(t_glob + 1, w_a).astype(F32)
        cnt_b = jnp.minimum(t_glob + 1, w_b).astype(F32)
        mean = jnp.where(first_group, sum_a / cnt_a, sum_b / cnt_b)
        parts.append(mean - pbuf_ref[MAX_WIN:MAX_WIN + SUBTILE, col:col + LANES])
    return jnp.concatenate(parts, axis=1)


def _mixer_kernel(x_ref, g_ref, win_ref, lbp_ref, ng_ref, mask_p_ref, mask_d_ref,
                  pw_ref, psc_ref, wout_ref, o_ref,
                  u_refs, hn_refs, y_refs, lk_refs, st_ref, pbuf_ref, *, ts):
    s_idx = pl.program_id(1)
    n_sub = ts // SUBTILE
    chunks = SUBTILE // CHUNK
    n_pieces = IN_WIDTH // PROJ_PIECE

    @pl.when(s_idx == 0)
    def _():
        st_ref[...] = jnp.zeros(st_ref.shape, F32)
        pbuf_ref[0:MAX_WIN, :] = jnp.zeros((MAX_WIN, POOL_WIDTH), F32)

    def norm_rows(sub, hn_ref):
        rows = pl.ds(pl.multiple_of(sub * SUBTILE, SUBTILE), SUBTILE)
        hn_ref[...] = _rms(x_ref[rows, :], g_ref[...]).astype(BF16)

    def in_proj_piece(hn_ref, u_ref, k):
        cols = slice(k * PROJ_PIECE, (k + 1) * PROJ_PIECE)
        u_ref[:, cols] = jnp.dot(hn_ref[...], win_ref[:, cols], preferred_element_type=F32)

    def col(base, h):
        return slice(base + h * HEAD_DIM, base + (h + 1) * HEAD_DIM)

    mask_p = mask_p_ref[...]
    mask_d = mask_d_ref[...]

    def step(sub, half):
        u, u_next = u_refs[half], u_refs[1 - half]
        hn_ref, y_ref, lk_ref = hn_refs[half], y_refs[half], lk_refs[half]
        norm_rows(jnp.minimum(sub + 1, n_sub - 1), hn_ref)
        pieces = iter(range(n_pieces))

        pbuf_ref[MAX_WIN:MAX_WIN + SUBTILE, :] = u[:, POOL_COL:POOL_COL + POOL_WIDTH]
        feats = _pool_features(pbuf_ref, s_idx * ts + sub * SUBTILE).astype(BF16)
        y_pool = jnp.dot(feats, pw_ref[...], preferred_element_type=F32) * psc_ref[...]
        y_ref[:, 0:POOL_WIDTH] = y_pool.astype(BF16)
        pbuf_ref[0:MAX_WIN, :] = pbuf_ref[SUBTILE:SUBTILE + MAX_WIN, :]
        in_proj_piece(hn_ref, u_next, next(pieces))

        st = [st_ref[h] for h in range(HEADS)]
        chunk_rows = [slice(c * CHUNK, (c + 1) * CHUNK) for c in range(chunks)]
        stage1 = []
        for c, rows in enumerate(chunk_rows):
            for h in range(HEADS):
                a, st[h] = _hgrn_scores(
                    u[rows, col(Q_COL, h)], u[rows, col(F_COL, h)], u[rows, col(I_COL, h)],
                    lbp_ref[h, 0:1, :], lbp_ref[h, 1:2, :], lbp_ref[h, 2:3, :],
                    st[h], mask_d, lk_ref.at[c, h])
                stage1.append(a)
                in_proj_piece(hn_ref, u_next, next(pieces))
        for h in range(HEADS):
            st_ref[h] = st[h]
        for k in pieces:
            in_proj_piece(hn_ref, u_next, k)
        stage2 = [_hgrn_mix(a, mask_p) for a in stage1]
        for c, rows in enumerate(chunk_rows):
            for h in range(HEADS):
                y = _hgrn_out(stage2[c * HEADS + h], u[rows, col(G_COL, h)], ng_ref[h])
                y_ref[rows, col(POOL_WIDTH, h)] = y.astype(BF16)

        rows = pl.ds(pl.multiple_of(sub * SUBTILE, SUBTILE), SUBTILE)
        o_ref[rows, :] = x_ref[rows, :] + jnp.dot(y_ref[...], wout_ref[...],
                                                  preferred_element_type=F32)

    norm_rows(0, hn_refs[1])
    for k in range(n_pieces):
        in_proj_piece(hn_refs[1], u_refs[0], k)

    def body(pair, carry):
        step(2 * pair, 0)
        step(2 * pair + 1, 1)
        return carry

    lax.fori_loop(0, n_sub // 2, body, 0)


def _mixer(x3, g, w_in_bf, lb_params, norm_g, pool_w_bd, pool_scale, w_out_bf, ts):
    b, s, d = x3.shape
    mask_p, mask_d = _score_masks()
    const2 = lambda shape: pl.BlockSpec(shape, lambda bi, si: (0, 0))
    const3 = lambda shape: pl.BlockSpec(shape, lambda bi, si: (0, 0, 0))
    return pl.pallas_call(
        functools.partial(_mixer_kernel, ts=ts),
        grid=(b, s // ts),
        in_specs=[
            pl.BlockSpec((None, ts, d), lambda bi, si: (bi, si, 0)),
            const2((1, d)),
            const2((d, IN_WIDTH)),
            const3((HEADS, 8, HEAD_DIM)),
            const3((HEADS, 1, HEAD_DIM)),
            const2((CHUNK, STACK_PAD)),
            const2((CHUNK, LANES)),
            const2((POOL_WIDTH, POOL_WIDTH)),
            const2((1, POOL_WIDTH)),
            const2((d, d)),
        ],
        out_specs=pl.BlockSpec((None, ts, d), lambda bi, si: (bi, si, 0)),
        out_shape=jax.ShapeDtypeStruct((b, s, d), F32),
        scratch_shapes=[
            [pltpu.VMEM((SUBTILE, IN_WIDTH), F32)] * 2,
            [pltpu.VMEM((SUBTILE, d), BF16)] * 2,
            [pltpu.VMEM((SUBTILE, d), BF16)] * 2,
            [pltpu.VMEM((SUBTILE // CHUNK, HEADS, CHUNK, LANES), F32)] * 2,
            pltpu.VMEM((HEADS, HEAD_DIM, HEAD_DIM), F32),
            pltpu.VMEM((SUBTILE + MAX_WIN, POOL_WIDTH), F32),
        ],
        compiler_params=pltpu.CompilerParams(
            dimension_semantics=("arbitrary", "arbitrary"), vmem_limit_bytes=VMEM_LIMIT),
        name="mixer",
    )(x3, g, w_in_bf, lb_params, norm_g, mask_p, mask_d, pool_w_bd, pool_scale, w_out_bf)


def _mlp_kernel(x_ref, g_ref, wu_ref, wd_ref, fg_ref, o_ref, *, ff_tile, final_norm):
    x = x_ref[...]
    h = _rms(x, g_ref[...]).astype(BF16)
    acc = x
    for n in range(D_FF // ff_tile):
        cols = slice(n * ff_tile, (n + 1) * ff_tile)
        a = jnp.dot(h, wu_ref[:, cols], preferred_element_type=F32)
        a = jnp.square(jnp.maximum(a, 0.0)).astype(BF16)
        acc = acc + jnp.dot(a, wd_ref[cols, :], preferred_element_type=F32)
    o_ref[...] = _rms(acc, fg_ref[...]) if final_norm else acc


def _mlp(x2, g, wu_bf, wd_bf, final_g, tm, final_norm, ff_tile=1024):
    t = x2.shape[0]
    return pl.pallas_call(
        functools.partial(_mlp_kernel, ff_tile=ff_tile, final_norm=final_norm),
        grid=(t // tm,),
        in_specs=[
            pl.BlockSpec((tm, D_MODEL), lambda i: (i, 0)),
            pl.BlockSpec((1, D_MODEL), lambda i: (0, 0)),
            pl.BlockSpec((D_MODEL, D_FF), lambda i: (0, 0)),
            pl.BlockSpec((D_FF, D_MODEL), lambda i: (0, 0)),
            pl.BlockSpec((1, D_MODEL), lambda i: (0, 0)),
        ],
        out_specs=pl.BlockSpec((tm, D_MODEL), lambda i: (i, 0)),
        out_shape=jax.ShapeDtypeStruct((t, D_MODEL), F32),
        compiler_params=pltpu.CompilerParams(
            dimension_semantics=("arbitrary",), vmem_limit_bytes=VMEM_LIMIT),
        name="mlp",
    )(x2, g, wu_bf, wd_bf, final_g)


def _block_diag(w):
    g, c, _ = w.shape
    eye = jnp.eye(g, dtype=w.dtype)
    return (eye[:, None, :, None] * w[:, :, None, :]).reshape(g * c, g * c)


def kernel(x, norm_mix_g, w_in, pool_w, pool_scale, hgrn_lb_logits, hgrn_norm_g, w_out,
           norm_mlp_g, w_up, w_down, final_norm_g):
    b, s, d = x.shape
    depth = w_in.shape[0]
    t = b * s
    tm = 512
    ts = 1024

    lb_cum = jnp.cumsum(jax.nn.softmax(hgrn_lb_logits.astype(F32), axis=0), axis=0)
    lower = lb_cum - lb_cum[0:1]
    lbp = jnp.stack([lower, 1.0 - lower, jnp.log1p(-lower)], axis=1)
    lbp = jnp.pad(lbp, ((0, 0), (0, 5), (0, 0)))
    lbp = lbp.reshape(depth, 8, HEADS, HEAD_DIM).transpose(0, 2, 1, 3)

    for l in range(depth):
        w_in_l = jnp.concatenate([w_in[l][:, POOL_WIDTH:], w_in[l][:, :POOL_WIDTH]], axis=1)
        x = _mixer(x, norm_mix_g[l][None, :], w_in_l.astype(BF16), lbp[l],
                   hgrn_norm_g[l].reshape(HEADS, 1, HEAD_DIM),
                   _block_diag(pool_w[l]).astype(BF16), pool_scale[l][None, :],
                   w_out[l].astype(BF16), ts)
        x = _mlp(x.reshape(t, d), norm_mlp_g[l][None, :], w_up[l].astype(BF16),
                 w_down[l].astype(BF16), final_norm_g[None, :], tm,
                 final_norm=(l == depth - 1)).reshape(b, s, d)
    return x
```

```python
import functools

import jax
import jax.numpy as jnp
from jax import lax
from jax.experimental import pallas as pl
from jax.experimental.pallas import tpu as pltpu

D_MODEL = 1024
POOL_WINDOWS = (2, 4, 8, 16)
POOL_CH = 64
POOL_WIDTH = len(POOL_WINDOWS) * POOL_CH
MAX_WIN = max(POOL_WINDOWS)
HEAD_DIM = 128
HGRN_WIDTH = D_MODEL - POOL_WIDTH
HEADS = HGRN_WIDTH // HEAD_DIM
IN_WIDTH = POOL_WIDTH + 4 * HGRN_WIDTH
D_FF = 4 * D_MODEL
RMS_EPS = 1e-5
LOG2_E = 1.4426950408889634

LANES = 128
SUBLANES = 8
CHUNK = 64
NBLK = CHUNK // SUBLANES
BF16_TILE_BLOCKS = 2
STACK_ORDER = tuple(range(1, NBLK))
SUBTILE = 2 * CHUNK
PROJ_PIECE = 256
WEIGHT_STAGE_ROWS = 128
STAGE_LAG = 3
Q_COL, F_COL, I_COL, G_COL, POOL_COL = (j * HGRN_WIDTH for j in range(5))
VMEM_LIMIT = 56 * 1024 * 1024

BF16 = jnp.bfloat16
F32 = jnp.float32


def _group_blocks(i):
    return -(-i // BF16_TILE_BLOCKS) * BF16_TILE_BLOCKS


STACK_PAD = SUBLANES * sum(_group_blocks(i) for i in STACK_ORDER)


def _rms(x, g):
    ms = jnp.mean(x * x, axis=-1, keepdims=True)
    return x * lax.rsqrt(ms + RMS_EPS) * g


def _neg_abs(x):
    return jnp.minimum(x, -x)


def _silu(x):
    h = 0.5 * x
    return h + h * jnp.tanh(h)


def _bcast_row(x, r):
    return jnp.broadcast_to(x[r:r + 1, :], x.shape)


def _load_weight_bf16(w_hbm, stage_ref, sem, store):
    rows_per = stage_ref.shape[1]
    n = w_hbm.shape[0] // rows_per

    def copy(i):
        return pltpu.make_async_copy(w_hbm.at[pl.ds(i * rows_per, rows_per), :],
                                     stage_ref.at[i % 2], sem.at[i % 2])

    copy(0).start()
    for i in range(n):
        if i + 1 < n:
            copy(i + 1).start()
        copy(i).wait()
        store(slice(i * rows_per, (i + 1) * rows_per), stage_ref[i % 2].astype(BF16))


def _hgrn_scores(qr, z, v, lb, one_m_lb, log_one_m_lb, st, mask_d, lk_ref):
    e = jnp.exp(_neg_abs(z))
    d = 1.0 + e
    r = 1.0 / d
    er = e * r
    pos = z >= 0.0
    sig_p = jnp.where(pos, r, er)
    sig_n = jnp.where(pos, er, r)
    log_sig = jnp.minimum(z, 0.0) - jnp.log(d)
    log_gate = log_one_m_lb + log_sig
    f = lb + one_m_lb * sig_p
    logf2 = jnp.maximum(jnp.log(f), log_gate) * LOG2_E
    kk = one_m_lb * sig_n
    log2_kk = (log_gate - z) * LOG2_E
    q = _silu(qr)

    row = lax.broadcasted_iota(jnp.int32, (SUBLANES, LANES), 0)
    lane = lax.broadcasted_iota(jnp.int32, (SUBLANES, LANES), 1)
    blk = lambda a, j: a[SUBLANES * j:SUBLANES * (j + 1), :]

    cb = []
    for j in range(NBLK):
        c = blk(logf2, j)
        for sh in (1, 2, 4):
            c = c + jnp.where(row >= sh, pltpu.roll(c, sh, 0), 0.0)
        cb.append(c)
    tot = [_bcast_row(c, SUBLANES - 1) for c in cb]
    carry = [jnp.zeros((SUBLANES, LANES), F32)]
    for j in range(NBLK):
        carry.append(carry[-1] + tot[j])

    qb = [blk(q, j) for j in range(NBLK)]
    kb = [blk(kk, j) for j in range(NBLK)]

    for j in range(NBLK):
        lk_ref[SUBLANES * j:SUBLANES * (j + 1), :] = blk(log2_kk, j) - cb[j]
    p_diag = []
    for j in range(NBLK):
        acc = jnp.zeros((SUBLANES, LANES), F32)
        for s in range(SUBLANES):
            lk_s = lk_ref[SUBLANES * j + s:SUBLANES * j + s + 1, :]
            a = qb[j] * jnp.exp2(cb[j] + jnp.broadcast_to(lk_s, (SUBLANES, LANES)))
            score = jnp.sum(a, axis=1, keepdims=True)
            acc = jnp.where(lane == SUBLANES * j + s, score, acc)
        p_diag.append(acc)
    p_diag = jnp.where(mask_d > 0.0, jnp.concatenate(p_diag, axis=0), 0.0)
    p_diag = p_diag[:, :CHUNK].astype(BF16)

    q_blk = jnp.concatenate([qb[j] * jnp.exp2(cb[j]) for j in range(NBLK)], axis=0)
    q_chunk = jnp.concatenate([qb[j] * jnp.exp2(cb[j] + carry[j]) for j in range(NBLK)], axis=0)
    k_hat = [kb[j] * jnp.exp2(tot[j] - cb[j]) for j in range(NBLK)]
    e_tot = [jnp.exp2(tot[j]) for j in range(NBLK)]
    cur = {}
    k_groups = {}
    for i in range(1, NBLK + 1):
        for j in range(i - 1):
            cur[j] = cur[j] * e_tot[i - 1]
        cur[i - 1] = k_hat[i - 1]
        if i < NBLK:
            k_groups[i] = [cur[j] for j in range(i)] + [k_hat[j] for j in range(i, _group_blocks(i))]
    k_chunk = jnp.concatenate([cur[j] for j in range(NBLK)], axis=0)
    v_bf = v.astype(BF16)
    k_stack = jnp.concatenate([p for i in STACK_ORDER for p in k_groups[i]], axis=0).astype(BF16)

    nt = (((1,), (1,)), ((), ()))
    tn = (((0,), (0,)), ((), ()))
    s_off = lax.dot_general(q_blk.astype(BF16), k_stack, nt, preferred_element_type=F32)
    o_int = lax.dot_general(q_chunk.astype(BF16), st.astype(BF16), nt,
                            preferred_element_type=F32)
    kv = lax.dot_general(v_bf, k_chunk.astype(BF16), tn, preferred_element_type=F32)
    st_new = st * jnp.exp2(carry[NBLK][0:1, :]) + kv
    return dict(s_off=s_off, o_int=o_int, p_diag=p_diag, v_bf=v_bf), st_new


def _hgrn_mix(a, mask_p):
    v_bf = a["v_bf"]
    v_stack = jnp.concatenate([v_bf[:SUBLANES * _group_blocks(i), :] for i in STACK_ORDER], axis=0)
    p_off = (a["s_off"] * mask_p).astype(BF16)
    o = jnp.dot(p_off, v_stack, preferred_element_type=F32)
    o = o + jnp.dot(a["p_diag"], v_bf, preferred_element_type=F32)
    return o + a["o_int"]


def _hgrn_out(o, gr, norm_g):
    return _rms(o, norm_g) * _silu(gr)


def _score_masks():
    col_blk = []
    for i in STACK_ORDER:
        col_blk += [i] * (SUBLANES * i) + [-1] * (SUBLANES * (_group_blocks(i) - i))
    col_blk = jnp.asarray(col_blk, jnp.int32)[None, :]
    t = jnp.arange(CHUNK, dtype=jnp.int32)[:, None]
    mask_p = (t // SUBLANES == col_blk).astype(F32)
    c = jnp.arange(LANES, dtype=jnp.int32)[None, :]
    mask_d = ((c // SUBLANES == t // SUBLANES) & (c % SUBLANES <= t % SUBLANES)).astype(F32)
    return mask_p, mask_d


def _pool_features(pbuf_ref, t0):
    t_glob = t0 + lax.broadcasted_iota(jnp.int32, (SUBTILE, LANES), 0)
    lane = lax.broadcasted_iota(jnp.int32, (SUBTILE, LANES), 1)
    first_group = lane < POOL_CH

    def window_sum(col, lo, hi):
        acc = None
        for d in range(lo, hi):
            v = pbuf_ref[MAX_WIN - d:MAX_WIN - d + SUBTILE, col:col + LANES]
            acc = v if acc is None else acc + v
        return acc

    parts = []
    for half in range(POOL_WIDTH // LANES):
        col = half * LANES
        w_a, w_b = POOL_WINDOWS[2 * half], POOL_WINDOWS[2 * half + 1]
        sum_a = window_sum(col, 0, w_a)
        sum_b = sum_a + window_sum(col, w_a, w_b)
        cnt_a = jnp.minimum(t_glob + 1, w_a).astype(F32)
        cnt_b = jnp.minimum(t_glob + 1, w_b).astype(F32)
        mean = jnp.where(first_group, sum_a / cnt_a, sum_b / cnt_b)
        parts.append(mean - pbuf_ref[MAX_WIN:MAX_WIN + SUBTILE, col:col + LANES])
    return jnp.concatenate(parts, axis=1)


def _mixer_kernel(x_ref, xn_ref, g_ref, win_hbm, lbp_ref, ng_ref, mask_p_ref, mask_d_ref,
                  pw_ref, psc_ref, wout_hbm, o_ref,
                  win_ref, wout_ref, win_stage, wout_stage, sems,
                  u_refs, hn_refs, y_refs, lk_refs, st_ref, pbuf_ref, *, ts, layer):
    s_idx = pl.program_id(1)
    n_sub = ts // SUBTILE
    chunks = SUBTILE // CHUNK
    n_pieces = IN_WIDTH // PROJ_PIECE

    @pl.when((pl.program_id(0) == 0) & (s_idx == 0))
    def _():
        def store_in(rows, w):
            win_ref[rows, 0:POOL_COL] = w[:, POOL_WIDTH:]
            win_ref[rows, POOL_COL:IN_WIDTH] = w[:, :POOL_WIDTH]

        def store_out(rows, w):
            wout_ref[rows, :] = w

        _load_weight_bf16(win_hbm.at[layer], win_stage, sems.at[0], store_in)
        _load_weight_bf16(wout_hbm.at[layer], wout_stage, sems.at[1], store_out)

    @pl.when(s_idx == 0)
    def _():
        st_ref[...] = jnp.zeros(st_ref.shape, F32)
        pbuf_ref[0:MAX_WIN, :] = jnp.zeros((MAX_WIN, POOL_WIDTH), F32)

    def norm_rows(sub, hn_ref, may_cross):
        rows = pl.ds(pl.multiple_of(jnp.minimum(sub, n_sub - 1) * SUBTILE, SUBTILE), SUBTILE)
        xs = x_ref[rows, :]
        if may_cross:
            xs = jnp.where(sub >= n_sub, xn_ref[...], xs)
        hn_ref[...] = _rms(xs, g_ref[...]).astype(BF16)

    def in_proj_piece(hn_ref, u_ref, k):
        cols = slice(k * PROJ_PIECE, (k + 1) * PROJ_PIECE)
        u_ref[:, cols] = jnp.dot(hn_ref[...], win_ref[:, cols], preferred_element_type=F32)

    def col(base, h):
        return slice(base + h * HEAD_DIM, base + (h + 1) * HEAD_DIM)

    mask_p = mask_p_ref[...]
    mask_d = mask_d_ref[...]

    chunk_rows = [slice(c * CHUNK, (c + 1) * CHUNK) for c in range(chunks)]
    items = [(c, h) for c in range(chunks) for h in range(HEADS)]

    def step(sub, half, st, filler):
        u, u_next = u_refs[half], u_refs[1 - half]
        hn_ref, y_ref, lk_ref = hn_refs[half], y_refs[half], lk_refs[half]
        filler = list(filler)
        per_item = -(-len(filler) // len(items))
        norm_rows(sub + 1, hn_ref, may_cross=(half == 1))
        pieces = iter(range(n_pieces))

        pbuf_ref[MAX_WIN:MAX_WIN + SUBTILE, :] = u[:, POOL_COL:POOL_COL + POOL_WIDTH]
        feats = _pool_features(pbuf_ref, s_idx * ts + sub * SUBTILE).astype(BF16)
        y_pool = jnp.dot(feats, pw_ref[...], preferred_element_type=F32) * psc_ref[...]
        y_ref[:, 0:POOL_WIDTH] = y_pool.astype(BF16)
        pbuf_ref[0:MAX_WIN, :] = pbuf_ref[SUBTILE:SUBTILE + MAX_WIN, :]
        in_proj_piece(hn_ref, u_next, next(pieces))

        stage1 = []
        for c, h in items:
            rows = chunk_rows[c]
            a, st[h] = _hgrn_scores(
                u[rows, col(Q_COL, h)], u[rows, col(F_COL, h)], u[rows, col(I_COL, h)],
                lbp_ref[h, 0:1, :], lbp_ref[h, 1:2, :], lbp_ref[h, 2:3, :],
                st[h], mask_d, lk_ref.at[c, h])
            stage1.append(a)
            in_proj_piece(hn_ref, u_next, next(pieces))
            for thunk in filler[:per_item]:
                thunk()
            del filler[:per_item]
        for k in pieces:
            in_proj_piece(hn_ref, u_next, k)

        stage2 = {}

        def mix(i):
            stage2[i] = _hgrn_mix(stage1[i], mask_p)

        def finish(i):
            c, h = items[i]
            y = _hgrn_out(stage2.pop(i), u[chunk_rows[c], col(G_COL, h)], ng_ref[h])
            y_ref[chunk_rows[c], col(POOL_WIDTH, h)] = y.astype(BF16)

        def out_proj():
            rows = pl.ds(pl.multiple_of(sub * SUBTILE, SUBTILE), SUBTILE)
            o_ref[rows, :] = x_ref[rows, :] + jnp.dot(y_ref[...], wout_ref[...],
                                                      preferred_element_type=F32)

        tail = []
        for i in range(len(items) + STAGE_LAG):
            if i < len(items):
                tail.append(functools.partial(mix, i))
            if i >= STAGE_LAG:
                tail.append(functools.partial(finish, i - STAGE_LAG))
        tail.append(out_proj)
        return tail

    @pl.when(s_idx == 0)
    def _():
        norm_rows(0, hn_refs[1], may_cross=False)
        for k in range(n_pieces):
            in_proj_piece(hn_refs[1], u_refs[0], k)

    def body(pair, carry):
        st = [st_ref[h] for h in range(HEADS)]
        tail = step(2 * pair, 0, st, [])
        tail = step(2 * pair + 1, 1, st, tail)
        for thunk in tail:
            thunk()
        for h in range(HEADS):
            st_ref[h] = st[h]
        return carry

    lax.fori_loop(0, n_sub // 2, body, 0)


def _mixer(x3, g, w_in, lb_params, norm_g, pool_w_bd, pool_scale, w_out, ts, layer):
    b, s, d = x3.shape
    mask_p, mask_d = _score_masks()
    n_sub = ts // SUBTILE
    last_sub = s // SUBTILE - 1
    const2 = lambda shape: pl.BlockSpec(shape, lambda bi, si: (0, 0))
    const3 = lambda shape: pl.BlockSpec(shape, lambda bi, si: (0, 0, 0))
    return pl.pallas_call(
        functools.partial(_mixer_kernel, ts=ts, layer=layer),
        grid=(b, s // ts),
        in_specs=[
            pl.BlockSpec((None, ts, d), lambda bi, si: (bi, si, 0)),
            pl.BlockSpec((None, SUBTILE, d),
                         lambda bi, si: (bi, jnp.minimum((si + 1) * n_sub, last_sub), 0)),
            const2((1, d)),
            pl.BlockSpec(memory_space=pl.ANY),
            const3((HEADS, 8, HEAD_DIM)),
            const3((HEADS, 1, HEAD_DIM)),
            const2((CHUNK, STACK_PAD)),
            const2((CHUNK, LANES)),
            const2((POOL_WIDTH, POOL_WIDTH)),
            const2((1, POOL_WIDTH)),
            pl.BlockSpec(memory_space=pl.ANY),
        ],
        out_specs=pl.BlockSpec((None, ts, d), lambda bi, si: (bi, si, 0)),
        out_shape=jax.ShapeDtypeStruct((b, s, d), F32),
        scratch_shapes=[
            pltpu.VMEM((d, IN_WIDTH), BF16),
            pltpu.VMEM((d, d), BF16),
            pltpu.VMEM((2, WEIGHT_STAGE_ROWS, IN_WIDTH), F32),
            pltpu.VMEM((2, WEIGHT_STAGE_ROWS, d), F32),
            pltpu.SemaphoreType.DMA((2, 2)),
            [pltpu.VMEM((SUBTILE, IN_WIDTH), F32)] * 2,
            [pltpu.VMEM((SUBTILE, d), BF16)] * 2,
            [pltpu.VMEM((SUBTILE, d), BF16)] * 2,
            [pltpu.VMEM((SUBTILE // CHUNK, HEADS, CHUNK, LANES), F32)] * 2,
            pltpu.VMEM((HEADS, HEAD_DIM, HEAD_DIM), F32),
            pltpu.VMEM((SUBTILE + MAX_WIN, POOL_WIDTH), F32),
        ],
        compiler_params=pltpu.CompilerParams(
            dimension_semantics=("arbitrary", "arbitrary"), vmem_limit_bytes=VMEM_LIMIT),
        name="mixer",
    )(x3, x3, g, w_in, lb_params, norm_g, mask_p, mask_d, pool_w_bd, pool_scale, w_out)


def _mlp_kernel(x_ref, g_ref, wu_hbm, wd_hbm, fg_ref, o_ref,
                wu_ref, wd_ref, wu_stage, wd_stage, sems, *, ff_tile, final_norm, layer):
    @pl.when(pl.program_id(0) == 0)
    def _():
        def store_up(rows, w):
            wu_ref[rows, :] = w

        def store_down(rows, w):
            wd_ref[rows, :] = w

        _load_weight_bf16(wu_hbm.at[layer], wu_stage, sems.at[0], store_up)
        _load_weight_bf16(wd_hbm.at[layer], wd_stage, sems.at[1], store_down)

    x = x_ref[...]
    h = _rms(x, g_ref[...]).astype(BF16)
    acc = x
    for n in range(D_FF // ff_tile):
        cols = slice(n * ff_tile, (n + 1) * ff_tile)
        a = jnp.dot(h, wu_ref[:, cols], preferred_element_type=F32)
        a = jnp.square(jnp.maximum(a, 0.0)).astype(BF16)
        acc = acc + jnp.dot(a, wd_ref[cols, :], preferred_element_type=F32)
    o_ref[...] = _rms(acc, fg_ref[...]) if final_norm else acc


def _mlp(x2, g, w_up, w_down, final_g, tm, layer, final_norm, ff_tile=1024):
    t = x2.shape[0]
    return pl.pallas_call(
        functools.partial(_mlp_kernel, ff_tile=ff_tile, final_norm=final_norm, layer=layer),
        grid=(t // tm,),
        in_specs=[
            pl.BlockSpec((tm, D_MODEL), lambda i: (i, 0)),
            pl.BlockSpec((1, D_MODEL), lambda i: (0, 0)),
            pl.BlockSpec(memory_space=pl.ANY),
            pl.BlockSpec(memory_space=pl.ANY),
            pl.BlockSpec((1, D_MODEL), lambda i: (0, 0)),
        ],
        out_specs=pl.BlockSpec((tm, D_MODEL), lambda i: (i, 0)),
        out_shape=jax.ShapeDtypeStruct((t, D_MODEL), F32),
        scratch_shapes=[
            pltpu.VMEM((D_MODEL, D_FF), BF16),
            pltpu.VMEM((D_FF, D_MODEL), BF16),
            pltpu.VMEM((2, WEIGHT_STAGE_ROWS, D_FF), F32),
            pltpu.VMEM((2, 4 * WEIGHT_STAGE_ROWS, D_MODEL), F32),
            pltpu.SemaphoreType.DMA((2, 2)),
        ],
        compiler_params=pltpu.CompilerParams(
            dimension_semantics=("arbitrary",), vmem_limit_bytes=VMEM_LIMIT),
        name="mlp",
    )(x2, g, w_up, w_down, final_g)


def _block_diag(w):
    g, c, _ = w.shape
    eye = jnp.eye(g, dtype=w.dtype)
    return (eye[:, None, :, None] * w[:, :, None, :]).reshape(g * c, g * c)


def kernel(x, norm_mix_g, w_in, pool_w, pool_scale, hgrn_lb_logits, hgrn_norm_g, w_out,
           norm_mlp_g, w_up, w_down, final_norm_g):
    b, s, d = x.shape
    depth = w_in.shape[0]
    t = b * s
    tm = 512
    ts = 1024

    lb_cum = jnp.cumsum(jax.nn.softmax(hgrn_lb_logits.astype(F32), axis=0), axis=0)
    lower = lb_cum - lb_cum[0:1]
    lbp = jnp.stack([lower, 1.0 - lower, jnp.log1p(-lower)], axis=1)
    lbp = jnp.pad(lbp, ((0, 0), (0, 5), (0, 0)))
    lbp = lbp.reshape(depth, 8, HEADS, HEAD_DIM).transpose(0, 2, 1, 3)

    for l in range(depth):
        x = _mixer(x, norm_mix_g[l][None, :], w_in, lbp[l],
                   hgrn_norm_g[l].reshape(HEADS, 1, HEAD_DIM),
                   _block_diag(pool_w[l]).astype(BF16), pool_scale[l][None, :],
                   w_out, ts, layer=l)
        x = _mlp(x.reshape(t, d), norm_mlp_g[l][None, :], w_up, w_down,
                 final_norm_g[None, :], tm, layer=l,
                 final_norm=(l == depth - 1)).reshape(b, s, d)
    return x
```

```python
import functools

import jax
import jax.numpy as jnp
from jax import lax
from jax.experimental import pallas as pl
from jax.experimental.pallas import tpu as pltpu

D_MODEL = 1024
POOL_WINDOWS = (2, 4, 8, 16)
POOL_CH = 64
POOL_WIDTH = len(POOL_WINDOWS) * POOL_CH
MAX_WIN = max(POOL_WINDOWS)
HEAD_DIM = 128
HGRN_WIDTH = D_MODEL - POOL_WIDTH
HEADS = HGRN_WIDTH // HEAD_DIM
IN_WIDTH = POOL_WIDTH + 4 * HGRN_WIDTH
D_FF = 4 * D_MODEL
RMS_EPS = 1e-5
LOG2_E = 1.4426950408889634

LANES = 128
SUBLANES = 8
CHUNK = 64
NBLK = CHUNK // SUBLANES
BF16_TILE_BLOCKS = 2
STACK_ORDER = tuple(range(1, NBLK))
SUBTILE = 2 * CHUNK
PROJ_PIECE = 256
WEIGHT_STAGE_ROWS = 128
STAGE_LAG = 3
Q_COL, F_COL, I_COL, G_COL, POOL_COL = (j * HGRN_WIDTH for j in range(5))
VMEM_LIMIT = 56 * 1024 * 1024

BF16 = jnp.bfloat16
F32 = jnp.float32


def _group_blocks(i):
    return -(-i // BF16_TILE_BLOCKS) * BF16_TILE_BLOCKS


STACK_PAD = SUBLANES * sum(_group_blocks(i) for i in STACK_ORDER)


def _rms(x, g):
    ms = jnp.mean(x * x, axis=-1, keepdims=True)
    return x * lax.rsqrt(ms + RMS_EPS) * g


def _neg_abs(x):
    return jnp.minimum(x, -x)


def _silu(x):
    h = 0.5 * x
    return h + h * jnp.tanh(h)


def _bcast_row(x, r):
    return jnp.broadcast_to(x[r:r + 1, :], x.shape)


def _load_weight_bf16(w_hbm, stage_ref, sem, store):
    rows_per = stage_ref.shape[1]
    n = w_hbm.shape[0] // rows_per

    def copy(i):
        return pltpu.make_async_copy(w_hbm.at[pl.ds(i * rows_per, rows_per), :],
                                     stage_ref.at[i % 2], sem.at[i % 2])

    copy(0).start()
    for i in range(n):
        if i + 1 < n:
            copy(i + 1).start()
        copy(i).wait()
        store(slice(i * rows_per, (i + 1) * rows_per), stage_ref[i % 2].astype(BF16))


def _hgrn_scores(qr, z, v, lb, one_m_lb, log_one_m_lb, st, mask_d, lk_ref):
    e = jnp.exp(_neg_abs(z))
    d = 1.0 + e
    r = 1.0 / d
    er = e * r
    pos = z >= 0.0
    sig_p = jnp.where(pos, r, er)
    sig_n = jnp.where(pos, er, r)
    log_sig = jnp.minimum(z, 0.0) - jnp.log(d)
    log_gate = log_one_m_lb + log_sig
    f = lb + one_m_lb * sig_p
    logf2 = jnp.maximum(jnp.log(f), log_gate) * LOG2_E
    kk = one_m_lb * sig_n
    log2_kk = (log_gate - z) * LOG2_E
    q = _silu(qr)

    row = lax.broadcasted_iota(jnp.int32, (SUBLANES, LANES), 0)
    lane = lax.broadcasted_iota(jnp.int32, (SUBLANES, LANES), 1)
    blk = lambda a, j: a[SUBLANES * j:SUBLANES * (j + 1), :]

    cb = []
    for j in range(NBLK):
        c = blk(logf2, j)
        for sh in (1, 2, 4):
            c = c + jnp.where(row >= sh, pltpu.roll(c, sh, 0), 0.0)
        cb.append(c)
    tot = [_bcast_row(c, SUBLANES - 1) for c in cb]
    carry = [jnp.zeros((SUBLANES, LANES), F32)]
    for j in range(NBLK):
        carry.append(carry[-1] + tot[j])

    qb = [blk(q, j) for j in range(NBLK)]
    kb = [blk(kk, j) for j in range(NBLK)]

    for j in range(NBLK):
        lk_ref[SUBLANES * j:SUBLANES * (j + 1), :] = blk(log2_kk, j) - cb[j]
    p_diag = []
    for j in range(NBLK):
        acc = jnp.zeros((SUBLANES, LANES), F32)
        for s in range(SUBLANES):
            lk_s = lk_ref[SUBLANES * j + s:SUBLANES * j + s + 1, :]
            a = qb[j] * jnp.exp2(cb[j] + jnp.broadcast_to(lk_s, (SUBLANES, LANES)))
            score = jnp.sum(a, axis=1, keepdims=True)
            acc = jnp.where(lane == SUBLANES * j + s, score, acc)
        p_diag.append(acc)
    p_diag = jnp.where(mask_d > 0.0, jnp.concatenate(p_diag, axis=0), 0.0)
    p_diag = p_diag[:, :CHUNK].astype(BF16)

    q_blk = jnp.concatenate([qb[j] * jnp.exp2(cb[j]) for j in range(NBLK)], axis=0)
    q_chunk = jnp.concatenate([qb[j] * jnp.exp2(cb[j] + carry[j]) for j in range(NBLK)], axis=0)
    k_hat = [kb[j] * jnp.exp2(tot[j] - cb[j]) for j in range(NBLK)]
    e_tot = [jnp.exp2(tot[j]) for j in range(NBLK)]
    cur = {}
    k_groups = {}
    for i in range(1, NBLK + 1):
        for j in range(i - 1):
            cur[j] = cur[j] * e_tot[i - 1]
        cur[i - 1] = k_hat[i - 1]
        if i < NBLK:
            k_groups[i] = [cur[j] for j in range(i)] + [k_hat[j] for j in range(i, _group_blocks(i))]
    k_chunk = jnp.concatenate([cur[j] for j in range(NBLK)], axis=0)
    v_bf = v.astype(BF16)
    k_stack = jnp.concatenate([p for i in STACK_ORDER for p in k_groups[i]], axis=0).astype(BF16)

    nt = (((1,), (1,)), ((), ()))
    tn = (((0,), (0,)), ((), ()))
    s_off = lax.dot_general(q_blk.astype(BF16), k_stack, nt, preferred_element_type=F32)
    o_int = lax.dot_general(q_chunk.astype(BF16), st.astype(BF16), nt,
                            preferred_element_type=F32)
    kv = lax.dot_general(v_bf, k_chunk.astype(BF16), tn, preferred_element_type=F32)
    st_new = st * jnp.exp2(carry[NBLK][0:1, :]) + kv
    return dict(s_off=s_off, o_int=o_int, p_diag=p_diag, v_bf=v_bf), st_new


def _hgrn_mix(a, mask_p):
    v_bf = a["v_bf"]
    v_stack = jnp.concatenate([v_bf[:SUBLANES * _group_blocks(i), :] for i in STACK_ORDER], axis=0)
    p_off = (a["s_off"] * mask_p).astype(BF16)
    o = jnp.dot(p_off, v_stack, preferred_element_type=F32)
    o = o + jnp.dot(a["p_diag"], v_bf, preferred_element_type=F32)
    return o + a["o_int"]


def _hgrn_out(o, gr, norm_g):
    return _rms(o, norm_g) * _silu(gr)


def _score_masks():
    col_blk = []
    for i in STACK_ORDER:
        col_blk += [i] * (SUBLANES * i) + [-1] * (SUBLANES * (_group_blocks(i) - i))
    col_blk = jnp.asarray(col_blk, jnp.int32)[None, :]
    t = jnp.arange(CHUNK, dtype=jnp.int32)[:, None]
    mask_p = (t // SUBLANES == col_blk).astype(F32)
    c = jnp.arange(LANES, dtype=jnp.int32)[None, :]
    mask_d = ((c // SUBLANES == t // SUBLANES) & (c % SUBLANES <= t % SUBLANES)).astype(F32)
    return mask_p, mask_d


def _pool_features(pbuf_ref, t0):
    t_glob = t0 + lax.broadcasted_iota(jnp.int32, (SUBTILE, LANES), 0)
    lane = lax.broadcasted_iota(jnp.int32, (SUBTILE, LANES), 1)
    first_group = lane < POOL_CH

    def window_sum(col, lo, hi):
        acc = None
        for d in range(lo, hi):
            v = pbuf_ref[MAX_WIN - d:MAX_WIN - d + SUBTILE, col:col + LANES]
            acc = v if acc is None else acc + v
        return acc

    parts = []
    for half in range(POOL_WIDTH // LANES):
        col = half * LANES
        w_a, w_b = POOL_WINDOWS[2 * half], POOL_WINDOWS[2 * half + 1]
        sum_a = window_sum(col, 0, w_a)
        sum_b = sum_a + window_sum(col, w_a, w_b)
        cnt_a = jnp.minimum(t_glob + 1, w_a).astype(F32)
        cnt_b = jnp.minimum(t_glob + 1, w_b).astype(F32)
        mean = jnp.where(first_group, sum_a / cnt_a, sum_b / cnt_b)
        parts.append(mean - pbuf_ref[MAX_WIN:MAX_WIN + SUBTILE, col:col + LANES])
    return jnp.concatenate(parts, axis=1)


def _mixer_kernel(x_ref, xn_ref, g_ref, win_hbm, lbp_ref, ng_ref, mask_p_ref, mask_d_ref,
                  pw_ref, psc_ref, wout_hbm, o_ref,
                  win_ref, wout_ref, win_stage, wout_stage, sems,
                  u_refs, hn_refs, y_refs, lk_refs, st_ref, pbuf_ref, *, ts, layer):
    s_idx = pl.program_id(1)
    n_sub = ts // SUBTILE
    chunks = SUBTILE // CHUNK
    n_pieces = IN_WIDTH // PROJ_PIECE

    @pl.when((pl.program_id(0) == 0) & (s_idx == 0))
    def _():
        def store_in(rows, w):
            win_ref[rows, 0:POOL_COL] = w[:, POOL_WIDTH:]
            win_ref[rows, POOL_COL:IN_WIDTH] = w[:, :POOL_WIDTH]

        def store_out(rows, w):
            wout_ref[rows, :] = w

        _load_weight_bf16(win_hbm.at[layer], win_stage, sems[0], store_in)
        _load_weight_bf16(wout_hbm.at[layer], wout_stage, sems[1], store_out)

    @pl.when(s_idx == 0)
    def _():
        st_ref[...] = jnp.zeros(st_ref.shape, F32)
        pbuf_ref[0:MAX_WIN, :] = jnp.zeros((MAX_WIN, POOL_WIDTH), F32)

    def norm_rows(sub, hn_ref, may_cross):
        rows = pl.ds(pl.multiple_of(jnp.minimum(sub, n_sub - 1) * SUBTILE, SUBTILE), SUBTILE)
        xs = x_ref[rows, :]
        if may_cross:
            xs = jnp.where(sub >= n_sub, xn_ref[...], xs)
        hn_ref[...] = _rms(xs, g_ref[...]).astype(BF16)

    def in_proj_piece(hn_ref, u_ref, k):
        cols = slice(k * PROJ_PIECE, (k + 1) * PROJ_PIECE)
        u_ref[:, cols] = jnp.dot(hn_ref[...], win_ref[:, cols], preferred_element_type=F32)

    def col(base, h):
        return slice(base + h * HEAD_DIM, base + (h + 1) * HEAD_DIM)


    chunk_rows = [slice(c * CHUNK, (c + 1) * CHUNK) for c in range(chunks)]
    items = [(c, h) for c in range(chunks) for h in range(HEADS)]

    def step(sub, half, st, filler):
        u, u_next = u_refs[half], u_refs[1 - half]
        hn_ref, y_ref, lk_ref = hn_refs[half], y_refs[half], lk_refs[half]
        filler = list(filler)
        per_item = -(-len(filler) // len(items))
        norm_rows(sub + 1, hn_ref, may_cross=(half == 1))
        pieces = iter(range(n_pieces))

        pbuf_ref[MAX_WIN:MAX_WIN + SUBTILE, :] = u[:, POOL_COL:POOL_COL + POOL_WIDTH]
        feats = _pool_features(pbuf_ref, s_idx * ts + sub * SUBTILE).astype(BF16)
        y_pool = jnp.dot(feats, pw_ref[...], preferred_element_type=F32) * psc_ref[...]
        y_ref[:, 0:POOL_WIDTH] = y_pool.astype(BF16)
        pbuf_ref[0:MAX_WIN, :] = pbuf_ref[SUBTILE:SUBTILE + MAX_WIN, :]
        in_proj_piece(hn_ref, u_next, next(pieces))

        stage1 = []
        for c, h in items:
            rows = chunk_rows[c]
            in_proj_piece(hn_ref, u_next, next(pieces))
            for thunk in filler[:per_item]:
                thunk()
            del filler[:per_item]
            a, st[h] = _hgrn_scores(
                u[rows, col(Q_COL, h)], u[rows, col(F_COL, h)], u[rows, col(I_COL, h)],
                lbp_ref[h, 0:1, :], lbp_ref[h, 1:2, :], lbp_ref[h, 2:3, :],
                st[h], mask_d_ref[...], lk_ref.at[c, h])
            stage1.append(a)
        for k in pieces:
            in_proj_piece(hn_ref, u_next, k)

        stage2 = {}

        def mix(i):
            stage2[i] = _hgrn_mix(stage1[i], mask_p_ref[...])

        def finish(i):
            c, h = items[i]
            y = _hgrn_out(stage2.pop(i), u[chunk_rows[c], col(G_COL, h)], ng_ref[h])
            y_ref[chunk_rows[c], col(POOL_WIDTH, h)] = y.astype(BF16)

        def out_proj():
            rows = pl.ds(pl.multiple_of(sub * SUBTILE, SUBTILE), SUBTILE)
            o_ref[rows, :] = x_ref[rows, :] + jnp.dot(y_ref[...], wout_ref[...],
                                                      preferred_element_type=F32)

        tail = []
        for i in range(len(items) + STAGE_LAG):
            if i < len(items):
                tail.append(functools.partial(mix, i))
            if i >= STAGE_LAG:
                tail.append(functools.partial(finish, i - STAGE_LAG))
        tail.append(out_proj)
        return tail

    @pl.when(s_idx == 0)
    def _():
        norm_rows(0, hn_refs[1], may_cross=False)
        for k in range(n_pieces):
            in_proj_piece(hn_refs[1], u_refs[0], k)

    def body(pair, carry):
        st = [st_ref[h] for h in range(HEADS)]
        tail = step(2 * pair, 0, st, [])
        tail = step(2 * pair + 1, 1, st, tail)
        for thunk in tail:
            thunk()
        for h in range(HEADS):
            st_ref[h] = st[h]
        return carry

    lax.fori_loop(0, n_sub // 2, body, 0)


def _mixer(x3, g, w_in, lb_params, norm_g, pool_w_bd, pool_scale, w_out, ts, layer):
    b, s, d = x3.shape
    mask_p, mask_d = _score_masks()
    n_sub = ts // SUBTILE
    last_sub = s // SUBTILE - 1
    const2 = lambda shape: pl.BlockSpec(shape, lambda bi, si: (0, 0))
    const3 = lambda shape: pl.BlockSpec(shape, lambda bi, si: (0, 0, 0))
    return pl.pallas_call(
        functools.partial(_mixer_kernel, ts=ts, layer=layer),
        grid=(b, s // ts),
        in_specs=[
            pl.BlockSpec((None, ts, d), lambda bi, si: (bi, si, 0)),
            pl.BlockSpec((None, SUBTILE, d),
                         lambda bi, si: (bi, jnp.minimum((si + 1) * n_sub, last_sub), 0)),
            const2((1, d)),
            pl.BlockSpec(memory_space=pl.ANY),
            const3((HEADS, 8, HEAD_DIM)),
            const3((HEADS, 1, HEAD_DIM)),
            const2((CHUNK, STACK_PAD)),
            const2((CHUNK, LANES)),
            const2((POOL_WIDTH, POOL_WIDTH)),
            const2((1, POOL_WIDTH)),
            pl.BlockSpec(memory_space=pl.ANY),
        ],
        out_specs=pl.BlockSpec((None, ts, d), lambda bi, si: (bi, si, 0)),
        out_shape=jax.ShapeDtypeStruct((b, s, d), F32),
        scratch_shapes=[
            pltpu.VMEM((d, IN_WIDTH), BF16),
            pltpu.VMEM((d, d), BF16),
            pltpu.VMEM((2, WEIGHT_STAGE_ROWS, IN_WIDTH), F32),
            pltpu.VMEM((2, WEIGHT_STAGE_ROWS, d), F32),
            [pltpu.SemaphoreType.DMA((2,))] * 2,
            [pltpu.VMEM((SUBTILE, IN_WIDTH), F32)] * 2,
            [pltpu.VMEM((SUBTILE, d), BF16)] * 2,
            [pltpu.VMEM((SUBTILE, d), BF16)] * 2,
            [pltpu.VMEM((SUBTILE // CHUNK, HEADS, CHUNK, LANES), F32)] * 2,
            pltpu.VMEM((HEADS, HEAD_DIM, HEAD_DIM), F32),
            pltpu.VMEM((SUBTILE + MAX_WIN, POOL_WIDTH), F32),
        ],
        compiler_params=pltpu.CompilerParams(
            dimension_semantics=("arbitrary", "arbitrary"), vmem_limit_bytes=VMEM_LIMIT),
        name="mixer",
    )(x3, x3, g, w_in, lb_params, norm_g, mask_p, mask_d, pool_w_bd, pool_scale, w_out)


def _mlp_kernel(x_ref, g_ref, wu_hbm, wd_hbm, fg_ref, o_ref,
                wu_ref, wd_ref, wu_stage, wd_stage, sems, *, ff_tile, final_norm, layer):
    @pl.when(pl.program_id(0) == 0)
    def _():
        def store_up(rows, w):
            wu_ref[rows, :] = w

        def store_down(rows, w):
            wd_ref[rows, :] = w

        _load_weight_bf16(wu_hbm.at[layer], wu_stage, sems[0], store_up)
        _load_weight_bf16(wd_hbm.at[layer], wd_stage, sems[1], store_down)

    x = x_ref[...]
    h = _rms(x, g_ref[...]).astype(BF16)
    acc = x
    for n in range(D_FF // ff_tile):
        cols = slice(n * ff_tile, (n + 1) * ff_tile)
        a = jnp.dot(h, wu_ref[:, cols], preferred_element_type=F32)
        a = jnp.square(jnp.maximum(a, 0.0)).astype(BF16)
        acc = acc + jnp.dot(a, wd_ref[cols, :], preferred_element_type=F32)
    o_ref[...] = _rms(acc, fg_ref[...]) if final_norm else acc


def _mlp(x2, g, w_up, w_down, final_g, tm, layer, final_norm, ff_tile=1024):
    t = x2.shape[0]
    return pl.pallas_call(
        functools.partial(_mlp_kernel, ff_tile=ff_tile, final_norm=final_norm, layer=layer),
        grid=(t // tm,),
        in_specs=[
            pl.BlockSpec((tm, D_MODEL), lambda i: (i, 0)),
            pl.BlockSpec((1, D_MODEL), lambda i: (0, 0)),
            pl.BlockSpec(memory_space=pl.ANY),
            pl.BlockSpec(memory_space=pl.ANY),
            pl.BlockSpec((1, D_MODEL), lambda i: (0, 0)),
        ],
        out_specs=pl.BlockSpec((tm, D_MODEL), lambda i: (i, 0)),
        out_shape=jax.ShapeDtypeStruct((t, D_MODEL), F32),
        scratch_shapes=[
            pltpu.VMEM((D_MODEL, D_FF), BF16),
            pltpu.VMEM((D_FF, D_MODEL), BF16),
            pltpu.VMEM((2, WEIGHT_STAGE_ROWS, D_FF), F32),
            pltpu.VMEM((2, 4 * WEIGHT_STAGE_ROWS, D_MODEL), F32),
            [pltpu.SemaphoreType.DMA((2,))] * 2,
        ],
        compiler_params=pltpu.CompilerParams(
            dimension_semantics=("arbitrary",), vmem_limit_bytes=VMEM_LIMIT),
        name="mlp",
    )(x2, g, w_up, w_down, final_g)


def _block_diag(w):
    g, c, _ = w.shape
    eye = jnp.eye(g, dtype=w.dtype)
    return (eye[:, None, :, None] * w[:, :, None, :]).reshape(g * c, g * c)


def kernel(x, norm_mix_g, w_in, pool_w, pool_scale, hgrn_lb_logits, hgrn_norm_g, w_out,
           norm_mlp_g, w_up, w_down, final_norm_g):
    b, s, d = x.shape
    depth = w_in.shape[0]
    t = b * s
    tm = 512
    ts = 2048

    lb_cum = jnp.cumsum(jax.nn.softmax(hgrn_lb_logits.astype(F32), axis=0), axis=0)
    lower = lb_cum - lb_cum[0:1]
    lbp = jnp.stack([lower, 1.0 - lower, jnp.log1p(-lower)], axis=1)
    lbp = jnp.pad(lbp, ((0, 0), (0, 5), (0, 0)))
    lbp = lbp.reshape(depth, 8, HEADS, HEAD_DIM).transpose(0, 2, 1, 3)

    for l in range(depth):
        x = _mixer(x, norm_mix_g[l][None, :], w_in, lbp[l],
                   hgrn_norm_g[l].reshape(HEADS, 1, HEAD_DIM),
                   _block_diag(pool_w[l]).astype(BF16), pool_scale[l][None, :],
                   w_out, ts, layer=l)
        x = _mlp(x.reshape(t, d), norm_mlp_g[l][None, :], w_up, w_down,
                 final_norm_g[None, :], tm, layer=l,
                 final_norm=(l == depth - 1)).reshape(b, s, d)
    return x
```

```python
import functools

import jax
import jax.numpy as jnp
from jax import lax
from jax.experimental import pallas as pl
from jax.experimental.pallas import tpu as pltpu

D_MODEL = 1024
POOL_WINDOWS = (2, 4, 8, 16)
POOL_CH = 64
POOL_WIDTH = len(POOL_WINDOWS) * POOL_CH
MAX_WIN = max(POOL_WINDOWS)
HEAD_DIM = 128
HGRN_WIDTH = D_MODEL - POOL_WIDTH
HEADS = HGRN_WIDTH // HEAD_DIM
IN_WIDTH = POOL_WIDTH + 4 * HGRN_WIDTH
D_FF = 4 * D_MODEL
RMS_EPS = 1e-5
LOG2_E = 1.4426950408889634

LANES = 128
SUBLANES = 8
CHUNK = 64
NBLK = CHUNK // SUBLANES
BF16_TILE_BLOCKS = 2
STACK_ORDER = tuple(range(1, NBLK))
SUBTILE = 2 * CHUNK
PROJ_PIECE = 256
WEIGHT_STAGE_ROWS = 128
STAGE_LAG = 3
Q_COL, F_COL, I_COL, G_COL, POOL_COL = (j * HGRN_WIDTH for j in range(5))
VMEM_LIMIT = 56 * 1024 * 1024

BF16 = jnp.bfloat16
F32 = jnp.float32


def _group_blocks(i):
    return -(-i // BF16_TILE_BLOCKS) * BF16_TILE_BLOCKS


STACK_PAD = SUBLANES * sum(_group_blocks(i) for i in STACK_ORDER)


def _rms(x, g):
    ms = jnp.mean(x * x, axis=-1, keepdims=True)
    return x * lax.rsqrt(ms + RMS_EPS) * g


def _neg_abs(x):
    return jnp.minimum(x, -x)


def _silu(x):
    h = 0.5 * x
    return h + h * jnp.tanh(h)


def _bcast_row(x, r):
    return jnp.broadcast_to(x[r:r + 1, :], x.shape)


def _load_weight_bf16(w_hbm, stage_ref, sem, store):
    rows_per = stage_ref.shape[1]
    n = w_hbm.shape[0] // rows_per

    def copy(i):
        return pltpu.make_async_copy(w_hbm.at[pl.ds(i * rows_per, rows_per), :],
                                     stage_ref.at[i % 2], sem.at[i % 2])

    copy(0).start()
    for i in range(n):
        if i + 1 < n:
            copy(i + 1).start()
        copy(i).wait()
        store(slice(i * rows_per, (i + 1) * rows_per), stage_ref[i % 2].astype(BF16))


def _hgrn_scores(qr, z, v, lb, one_m_lb, log_one_m_lb, st, mask_d, lk_ref):
    e = jnp.exp(_neg_abs(z))
    d = 1.0 + e
    r = 1.0 / d
    er = e * r
    pos = z >= 0.0
    sig_p = jnp.where(pos, r, er)
    sig_n = jnp.where(pos, er, r)
    log_sig = jnp.minimum(z, 0.0) - jnp.log(d)
    log_gate = log_one_m_lb + log_sig
    f = lb + one_m_lb * sig_p
    logf2 = jnp.maximum(jnp.log(f), log_gate) * LOG2_E
    kk = one_m_lb * sig_n
    log2_kk = (log_gate - z) * LOG2_E
    q = _silu(qr)

    row = lax.broadcasted_iota(jnp.int32, (SUBLANES, LANES), 0)
    lane = lax.broadcasted_iota(jnp.int32, (SUBLANES, LANES), 1)
    blk = lambda a, j: a[SUBLANES * j:SUBLANES * (j + 1), :]

    cb = []
    for j in range(NBLK):
        c = blk(logf2, j)
        for sh in (1, 2, 4):
            c = c + jnp.where(row >= sh, pltpu.roll(c, sh, 0), 0.0)
        cb.append(c)
    tot = [_bcast_row(c, SUBLANES - 1) for c in cb]
    carry = [jnp.zeros((SUBLANES, LANES), F32)]
    for j in range(NBLK):
        carry.append(carry[-1] + tot[j])

    qb = [blk(q, j) for j in range(NBLK)]
    kb = [blk(kk, j) for j in range(NBLK)]
    for j in range(NBLK):
        lk_ref[SUBLANES * j:SUBLANES * (j + 1), :] = blk(log2_kk, j) - cb[j]

    q_blk = jnp.concatenate([qb[j] * jnp.exp2(cb[j]) for j in range(NBLK)], axis=0)
    q_chunk = jnp.concatenate([qb[j] * jnp.exp2(cb[j] + carry[j]) for j in range(NBLK)], axis=0)
    k_hat = [kb[j] * jnp.exp2(tot[j] - cb[j]) for j in range(NBLK)]
    e_tot = [jnp.exp2(tot[j]) for j in range(NBLK)]
    cur = {}
    k_groups = {}
    for i in range(1, NBLK + 1):
        for j in range(i - 1):
            cur[j] = cur[j] * e_tot[i - 1]
        cur[i - 1] = k_hat[i - 1]
        if i < NBLK:
            k_groups[i] = [cur[j] for j in range(i)] + [k_hat[j] for j in range(i, _group_blocks(i))]
    k_chunk = jnp.concatenate([cur[j] for j in range(NBLK)], axis=0)
    v_bf = v.astype(BF16)
    k_stack = jnp.concatenate([p for i in STACK_ORDER for p in k_groups[i]], axis=0).astype(BF16)

    nt = (((1,), (1,)), ((), ()))
    tn = (((0,), (0,)), ((), ()))
    s_off = lax.dot_general(q_blk.astype(BF16), k_stack, nt, preferred_element_type=F32)
    o_int = lax.dot_general(q_chunk.astype(BF16), st.astype(BF16), nt,
                            preferred_element_type=F32)
    kv = lax.dot_general(v_bf, k_chunk.astype(BF16), tn, preferred_element_type=F32)
    st_new = st * jnp.exp2(carry[NBLK][0:1, :]) + kv

    p_diag = []
    for j in range(NBLK):
        acc = jnp.zeros((SUBLANES, LANES), F32)
        for s in range(SUBLANES):
            lk_s = lk_ref[SUBLANES * j + s:SUBLANES * j + s + 1, :]
            a = qb[j] * jnp.exp2(cb[j] + jnp.broadcast_to(lk_s, (SUBLANES, LANES)))
            score = jnp.sum(a, axis=1, keepdims=True)
            acc = jnp.where(lane == SUBLANES * j + s, score, acc)
        p_diag.append(acc)
    p_diag = jnp.where(mask_d > 0.0, jnp.concatenate(p_diag, axis=0), 0.0)
    p_diag = p_diag[:, :CHUNK].astype(BF16)
    return dict(s_off=s_off, o_int=o_int, p_diag=p_diag, v_bf=v_bf), st_new


def _hgrn_mix(a, mask_p):
    v_bf = a["v_bf"]
    v_stack = jnp.concatenate([v_bf[:SUBLANES * _group_blocks(i), :] for i in STACK_ORDER], axis=0)
    p_off = (a["s_off"] * mask_p).astype(BF16)
    o = jnp.dot(p_off, v_stack, preferred_element_type=F32)
    o = o + jnp.dot(a["p_diag"], v_bf, preferred_element_type=F32)
    return o + a["o_int"]


def _hgrn_out(o, gr, norm_g):
    return _rms(o, norm_g) * _silu(gr)


def _score_masks():
    col_blk = []
    for i in STACK_ORDER:
        col_blk += [i] * (SUBLANES * i) + [-1] * (SUBLANES * (_group_blocks(i) - i))
    col_blk = jnp.asarray(col_blk, jnp.int32)[None, :]
    t = jnp.arange(CHUNK, dtype=jnp.int32)[:, None]
    mask_p = (t // SUBLANES == col_blk).astype(F32)
    c = jnp.arange(LANES, dtype=jnp.int32)[None, :]
    mask_d = ((c // SUBLANES == t // SUBLANES) & (c % SUBLANES <= t % SUBLANES)).astype(F32)
    return mask_p, mask_d


def _pool_features(pbuf_ref, t0):
    t_glob = t0 + lax.broadcasted_iota(jnp.int32, (SUBTILE, LANES), 0)
    lane = lax.broadcasted_iota(jnp.int32, (SUBTILE, LANES), 1)
    first_group = lane < POOL_CH

    def window_sum(col, lo, hi):
        acc = None
        for d in range(lo, hi):
            v = pbuf_ref[MAX_WIN - d:MAX_WIN - d + SUBTILE, col:col + LANES]
            acc = v if acc is None else acc + v
        return acc

    parts = []
    for half in range(POOL_WIDTH // LANES):
        col = half * LANES
        w_a, w_b = POOL_WINDOWS[2 * half], POOL_WINDOWS[2 * half + 1]
        sum_a = window_sum(col, 0, w_a)
        sum_b = sum_a + window_sum(col, w_a, w_b)
        cnt_a = jnp.minimum(t_glob + 1, w_a).astype(F32)
        cnt_b = jnp.minimum(t_glob + 1, w_b).astype(F32)
        mean = jnp.where(first_group, sum_a / cnt_a, sum_b / cnt_b)
        parts.append(mean - pbuf_ref[MAX_WIN:MAX_WIN + SUBTILE, col:col + LANES])
    return jnp.concatenate(parts, axis=1)


def _mixer_kernel(x_ref, xn_ref, g_ref, win_hbm, lbp_ref, ng_ref, mask_p_ref, mask_d_ref,
                  pw_ref, psc_ref, wout_hbm, o_ref,
                  win_ref, wout_ref, win_stage, wout_stage, sems,
                  u_refs, hn_refs, y_refs, lk_refs, st_ref, pbuf_ref, *, ts, layer):
    s_idx = pl.program_id(1)
    n_sub = ts // SUBTILE
    chunks = SUBTILE // CHUNK
    piece_cols = [slice(lo, min(lo + PROJ_PIECE, IN_WIDTH)) for lo in range(0, IN_WIDTH, PROJ_PIECE)]
    n_pieces = len(piece_cols)

    @pl.when((pl.program_id(0) == 0) & (s_idx == 0))
    def _():
        def store_in(rows, w):
            win_ref[rows, 0:POOL_COL] = w[:, POOL_WIDTH:]
            win_ref[rows, POOL_COL:IN_WIDTH] = w[:, :POOL_WIDTH]

        def store_out(rows, w):
            wout_ref[rows, :] = w

        _load_weight_bf16(win_hbm.at[layer], win_stage, sems[0], store_in)
        _load_weight_bf16(wout_hbm.at[layer], wout_stage, sems[1], store_out)

    @pl.when(s_idx == 0)
    def _():
        st_ref[...] = jnp.zeros(st_ref.shape, F32)
        pbuf_ref[0:MAX_WIN, :] = jnp.zeros((MAX_WIN, POOL_WIDTH), F32)

    def norm_rows(sub, hn_ref, may_cross):
        rows = pl.ds(pl.multiple_of(jnp.minimum(sub, n_sub - 1) * SUBTILE, SUBTILE), SUBTILE)
        xs = x_ref[rows, :]
        if may_cross:
            xs = jnp.where(sub >= n_sub, xn_ref[...], xs)
        hn_ref[...] = _rms(xs, g_ref[...]).astype(BF16)

    def in_proj_piece(hn_ref, u_ref, k):
        cols = piece_cols[k]
        u_ref[:, cols] = jnp.dot(hn_ref[...], win_ref[:, cols], preferred_element_type=F32)

    def col(base, h):
        return slice(base + h * HEAD_DIM, base + (h + 1) * HEAD_DIM)


    chunk_rows = [slice(c * CHUNK, (c + 1) * CHUNK) for c in range(chunks)]
    items = [(c, h) for c in range(chunks) for h in range(HEADS)]

    def step(sub, half, st, filler):
        u, u_next = u_refs[half], u_refs[1 - half]
        hn_ref, y_ref, lk_ref = hn_refs[half], y_refs[half], lk_refs[half]
        filler = list(filler)
        per_item = -(-len(filler) // len(items))
        norm_rows(sub + 1, hn_ref, may_cross=(half == 1))
        n_slots = len(items) + 1

        def emit_pieces(slot):
            for k in range(n_pieces):
                if k * n_slots // n_pieces == slot:
                    in_proj_piece(hn_ref, u_next, k)

        pbuf_ref[MAX_WIN:MAX_WIN + SUBTILE, :] = u[:, POOL_COL:POOL_COL + POOL_WIDTH]
        feats = _pool_features(pbuf_ref, s_idx * ts + sub * SUBTILE).astype(BF16)
        y_pool = jnp.dot(feats, pw_ref[...], preferred_element_type=F32) * psc_ref[...]
        y_ref[:, 0:POOL_WIDTH] = y_pool.astype(BF16)
        pbuf_ref[0:MAX_WIN, :] = pbuf_ref[SUBTILE:SUBTILE + MAX_WIN, :]
        emit_pieces(0)

        stage1 = []
        for slot, (c, h) in enumerate(items, start=1):
            rows = chunk_rows[c]
            emit_pieces(slot)
            for thunk in filler[:per_item]:
                thunk()
            del filler[:per_item]
            a, st[h] = _hgrn_scores(
                u[rows, col(Q_COL, h)], u[rows, col(F_COL, h)], u[rows, col(I_COL, h)],
                lbp_ref[h, 0:1, :], lbp_ref[h, 1:2, :], lbp_ref[h, 2:3, :],
                st[h], mask_d_ref[...], lk_ref.at[c, h])
            a["gate"] = u[rows, col(G_COL, h)]
            stage1.append(a)

        stage2 = {}

        def mix(i):
            stage2[i] = _hgrn_mix(stage1[i], mask_p_ref[...])

        def finish(i):
            c, h = items[i]
            y = _hgrn_out(stage2.pop(i), stage1[i]["gate"], ng_ref[h])
            y_ref[chunk_rows[c], col(POOL_WIDTH, h)] = y.astype(BF16)

        def out_proj():
            rows = pl.ds(pl.multiple_of(sub * SUBTILE, SUBTILE), SUBTILE)
            o_ref[rows, :] = x_ref[rows, :] + jnp.dot(y_ref[...], wout_ref[...],
                                                      preferred_element_type=F32)

        tail = []
        for i in range(len(items) + STAGE_LAG):
            if i < len(items):
                tail.append(functools.partial(mix, i))
            if i >= STAGE_LAG:
                tail.append(functools.partial(finish, i - STAGE_LAG))
        tail.append(out_proj)
        return tail

    @pl.when(s_idx == 0)
    def _():
        norm_rows(0, hn_refs[1], may_cross=False)
        for k in range(n_pieces):
            in_proj_piece(hn_refs[1], u_refs[0], k)

    def body(pair, carry):
        st = [st_ref[h] for h in range(HEADS)]
        tail = step(2 * pair, 0, st, [])
        tail = step(2 * pair + 1, 1, st, tail)
        for thunk in tail:
            thunk()
        for h in range(HEADS):
            st_ref[h] = st[h]
        return carry

    lax.fori_loop(0, n_sub // 2, body, 0)


def _mixer(x3, g, w_in, lb_params, norm_g, pool_w_bd, pool_scale, w_out, ts, layer):
    b, s, d = x3.shape
    mask_p, mask_d = _score_masks()
    n_sub = ts // SUBTILE
    last_sub = s // SUBTILE - 1
    const2 = lambda shape: pl.BlockSpec(shape, lambda bi, si: (0, 0))
    const3 = lambda shape: pl.BlockSpec(shape, lambda bi, si: (0, 0, 0))
    return pl.pallas_call(
        functools.partial(_mixer_kernel, ts=ts, layer=layer),
        grid=(b, s // ts),
        in_specs=[
            pl.BlockSpec((None, ts, d), lambda bi, si: (bi, si, 0)),
            pl.BlockSpec((None, SUBTILE, d),
                         lambda bi, si: (bi, jnp.minimum((si + 1) * n_sub, last_sub), 0)),
            const2((1, d)),
            pl.BlockSpec(memory_space=pl.ANY),
            const3((HEADS, 8, HEAD_DIM)),
            const3((HEADS, 1, HEAD_DIM)),
            const2((CHUNK, STACK_PAD)),
            const2((CHUNK, LANES)),
            const2((POOL_WIDTH, POOL_WIDTH)),
            const2((1, POOL_WIDTH)),
            pl.BlockSpec(memory_space=pl.ANY),
        ],
        out_specs=pl.BlockSpec((None, ts, d), lambda bi, si: (bi, si, 0)),
        out_shape=jax.ShapeDtypeStruct((b, s, d), F32),
        scratch_shapes=[
            pltpu.VMEM((d, IN_WIDTH), BF16),
            pltpu.VMEM((d, d), BF16),
            pltpu.VMEM((2, WEIGHT_STAGE_ROWS, IN_WIDTH), F32),
            pltpu.VMEM((2, WEIGHT_STAGE_ROWS, d), F32),
            [pltpu.SemaphoreType.DMA((2,))] * 2,
            [pltpu.VMEM((SUBTILE, IN_WIDTH), F32)] * 2,
            [pltpu.VMEM((SUBTILE, d), BF16)] * 2,
            [pltpu.VMEM((SUBTILE, d), BF16)] * 2,
            [pltpu.VMEM((SUBTILE // CHUNK, HEADS, CHUNK, LANES), F32)] * 2,
            pltpu.VMEM((HEADS, HEAD_DIM, HEAD_DIM), F32),
            pltpu.VMEM((SUBTILE + MAX_WIN, POOL_WIDTH), F32),
        ],
        compiler_params=pltpu.CompilerParams(
            dimension_semantics=("arbitrary", "arbitrary"), vmem_limit_bytes=VMEM_LIMIT),
        name="mixer",
    )(x3, x3, g, w_in, lb_params, norm_g, mask_p, mask_d, pool_w_bd, pool_scale, w_out)


def _mlp_kernel(x_ref, g_ref, wu_hbm, wd_hbm, fg_ref, o_ref,
                wu_ref, wd_ref, wu_stage, wd_stage, sems, *, ff_tile, final_norm, layer):
    @pl.when(pl.program_id(0) == 0)
    def _():
        def store_up(rows, w):
            wu_ref[rows, :] = w

        def store_down(rows, w):
            wd_ref[rows, :] = w

        _load_weight_bf16(wu_hbm.at[layer], wu_stage, sems[0], store_up)
        _load_weight_bf16(wd_hbm.at[layer], wd_stage, sems[1], store_down)

    x = x_ref[...]
    h = _rms(x, g_ref[...]).astype(BF16)
    acc = x
    for n in range(D_FF // ff_tile):
        cols = slice(n * ff_tile, (n + 1) * ff_tile)
        a = jnp.dot(h, wu_ref[:, cols], preferred_element_type=F32)
        a = jnp.square(jnp.maximum(a, 0.0)).astype(BF16)
        acc = acc + jnp.dot(a, wd_ref[cols, :], preferred_element_type=F32)
    o_ref[...] = _rms(acc, fg_ref[...]) if final_norm else acc


def _mlp(x2, g, w_up, w_down, final_g, tm, layer, final_norm, ff_tile=1024):
    t = x2.shape[0]
    return pl.pallas_call(
        functools.partial(_mlp_kernel, ff_tile=ff_tile, final_norm=final_norm, layer=layer),
        grid=(t // tm,),
        in_specs=[
            pl.BlockSpec((tm, D_MODEL), lambda i: (i, 0)),
            pl.BlockSpec((1, D_MODEL), lambda i: (0, 0)),
            pl.BlockSpec(memory_space=pl.ANY),
            pl.BlockSpec(memory_space=pl.ANY),
            pl.BlockSpec((1, D_MODEL), lambda i: (0, 0)),
        ],
        out_specs=pl.BlockSpec((tm, D_MODEL), lambda i: (i, 0)),
        out_shape=jax.ShapeDtypeStruct((t, D_MODEL), F32),
        scratch_shapes=[
            pltpu.VMEM((D_MODEL, D_FF), BF16),
            pltpu.VMEM((D_FF, D_MODEL), BF16),
            pltpu.VMEM((2, WEIGHT_STAGE_ROWS, D_FF), F32),
            pltpu.VMEM((2, 4 * WEIGHT_STAGE_ROWS, D_MODEL), F32),
            [pltpu.SemaphoreType.DMA((2,))] * 2,
        ],
        compiler_params=pltpu.CompilerParams(
            dimension_semantics=("arbitrary",), vmem_limit_bytes=VMEM_LIMIT),
        name="mlp",
    )(x2, g, w_up, w_down, final_g)


def _block_diag(w):
    g, c, _ = w.shape
    eye = jnp.eye(g, dtype=w.dtype)
    return (eye[:, None, :, None] * w[:, :, None, :]).reshape(g * c, g * c)


def kernel(x, norm_mix_g, w_in, pool_w, pool_scale, hgrn_lb_logits, hgrn_norm_g, w_out,
           norm_mlp_g, w_up, w_down, final_norm_g):
    b, s, d = x.shape
    depth = w_in.shape[0]
    t = b * s
    tm = 512
    ts = 2048

    lb_cum = jnp.cumsum(jax.nn.softmax(hgrn_lb_logits.astype(F32), axis=0), axis=0)
    lower = lb_cum - lb_cum[0:1]
    lbp = jnp.stack([lower, 1.0 - lower, jnp.log1p(-lower)], axis=1)
    lbp = jnp.pad(lbp, ((0, 0), (0, 5), (0, 0)))
    lbp = lbp.reshape(depth, 8, HEADS, HEAD_DIM).transpose(0, 2, 1, 3)

    for l in range(depth):
        x = _mixer(x, norm_mix_g[l][None, :], w_in, lbp[l],
                   hgrn_norm_g[l].reshape(HEADS, 1, HEAD_DIM),
                   _block_diag(pool_w[l]).astype(BF16), pool_scale[l][None, :],
                   w_out, ts, layer=l)
        x = _mlp(x.reshape(t, d), norm_mlp_g[l][None, :], w_up, w_down,
                 final_norm_g[None, :], tm, layer=l,
                 final_norm=(l == depth - 1)).reshape(b, s, d)
    return x
```

```python
import functools

import jax
import jax.numpy as jnp
from jax import lax
from jax.experimental import pallas as pl
from jax.experimental.pallas import tpu as pltpu

D_MODEL = 1024
POOL_WINDOWS = (2, 4, 8, 16)
POOL_CH = 64
POOL_WIDTH = len(POOL_WINDOWS) * POOL_CH
MAX_WIN = max(POOL_WINDOWS)
HEAD_DIM = 128
HGRN_WIDTH = D_MODEL - POOL_WIDTH
HEADS = HGRN_WIDTH // HEAD_DIM
IN_WIDTH = POOL_WIDTH + 4 * HGRN_WIDTH
D_FF = 4 * D_MODEL
RMS_EPS = 1e-5
LOG2_E = 1.4426950408889634

LANES = 128
SUBLANES = 8
CHUNK = 64
NBLK = CHUNK // SUBLANES
BF16_TILE_BLOCKS = 2
STACK_ORDER = tuple(range(1, NBLK))
SUBTILE = 2 * CHUNK
PROJ_PIECE = 256
WEIGHT_STAGE_ROWS = 128
STAGE_LAG = 3
POOL_SLOT = 7
Q_COL, F_COL, I_COL, G_COL, POOL_COL = (j * HGRN_WIDTH for j in range(5))
VMEM_LIMIT = 56 * 1024 * 1024

BF16 = jnp.bfloat16
F32 = jnp.float32


def _group_blocks(i):
    return -(-i // BF16_TILE_BLOCKS) * BF16_TILE_BLOCKS


STACK_PAD = SUBLANES * sum(_group_blocks(i) for i in STACK_ORDER)


def _rms(x, g):
    ms = jnp.mean(x * x, axis=-1, keepdims=True)
    return x * lax.rsqrt(ms + RMS_EPS) * g


def _neg_abs(x):
    return jnp.minimum(x, -x)


def _silu(x):
    h = 0.5 * x
    return h + h * jnp.tanh(h)


def _bcast_row(x, r):
    return jnp.broadcast_to(x[r:r + 1, :], x.shape)


def _load_weight_bf16(w_hbm, stage_ref, sem, store):
    rows_per = stage_ref.shape[1]
    n = w_hbm.shape[0] // rows_per

    def copy(i):
        return pltpu.make_async_copy(w_hbm.at[pl.ds(i * rows_per, rows_per), :],
                                     stage_ref.at[i % 2], sem.at[i % 2])

    copy(0).start()
    for i in range(n):
        if i + 1 < n:
            copy(i + 1).start()
        copy(i).wait()
        store(slice(i * rows_per, (i + 1) * rows_per), stage_ref[i % 2].astype(BF16))


def _hgrn_scores(qr, z, v, lb, one_m_lb, log_one_m_lb, st, mask_d, lk_ref):
    e = jnp.exp(_neg_abs(z))
    d = 1.0 + e
    r = 1.0 / d
    sig_p = jnp.where(z >= 0.0, r, e * r)
    log_sig = jnp.minimum(z, 0.0) - jnp.log(d)
    log_gate = log_one_m_lb + log_sig
    f = lb + one_m_lb * sig_p
    logf2 = jnp.maximum(jnp.log(f), log_gate) * LOG2_E
    log2_kk = (log_gate - z) * LOG2_E
    q = _silu(qr)

    row = lax.broadcasted_iota(jnp.int32, (SUBLANES, LANES), 0)
    lane = lax.broadcasted_iota(jnp.int32, (SUBLANES, LANES), 1)
    blk = lambda a, j: a[SUBLANES * j:SUBLANES * (j + 1), :]

    cb = []
    for j in range(NBLK):
        c = blk(logf2, j)
        for sh in (1, 2, 4):
            c = c + jnp.where(row >= sh, pltpu.roll(c, sh, 0), 0.0)
        cb.append(c)
    tot = [_bcast_row(c, SUBLANES - 1) for c in cb]
    carry = [jnp.zeros((SUBLANES, LANES), F32)]
    for j in range(NBLK):
        carry.append(carry[-1] + tot[j])

    qb = [blk(q, j) for j in range(NBLK)]
    lk = [blk(log2_kk, j) - cb[j] for j in range(NBLK)]
    for j in range(NBLK):
        lk_ref[SUBLANES * j:SUBLANES * (j + 1), :] = lk[j]

    q_blk = jnp.concatenate([qb[j] * jnp.exp2(cb[j]) for j in range(NBLK)], axis=0)
    q_chunk = jnp.concatenate([qb[j] * jnp.exp2(cb[j] + carry[j]) for j in range(NBLK)], axis=0)
    k_hat = [jnp.exp2(lk[j] + tot[j]) for j in range(NBLK)]
    e_tot = [jnp.exp2(tot[j]) for j in range(NBLK)]
    cur = {}
    k_groups = {}
    for i in range(1, NBLK + 1):
        for j in range(i - 1):
            cur[j] = cur[j] * e_tot[i - 1]
        cur[i - 1] = k_hat[i - 1]
        if i < NBLK:
            k_groups[i] = [cur[j] for j in range(i)] + [k_hat[j] for j in range(i, _group_blocks(i))]
    k_chunk = jnp.concatenate([cur[j] for j in range(NBLK)], axis=0)
    v_bf = v.astype(BF16)
    k_stack = jnp.concatenate([p for i in STACK_ORDER for p in k_groups[i]], axis=0).astype(BF16)

    nt = (((1,), (1,)), ((), ()))
    tn = (((0,), (0,)), ((), ()))
    s_off = lax.dot_general(q_blk.astype(BF16), k_stack, nt, preferred_element_type=F32)
    o_int = lax.dot_general(q_chunk.astype(BF16), st.astype(BF16), nt,
                            preferred_element_type=F32)
    kv = lax.dot_general(v_bf, k_chunk.astype(BF16), tn, preferred_element_type=F32)
    st_new = st * jnp.exp2(carry[NBLK][0:1, :]) + kv

    p_diag = []
    for j in range(NBLK):
        acc = jnp.zeros((SUBLANES, LANES), F32)
        for s in range(SUBLANES):
            lk_s = lk_ref[SUBLANES * j + s:SUBLANES * j + s + 1, :]
            a = qb[j] * jnp.exp2(cb[j] + jnp.broadcast_to(lk_s, (SUBLANES, LANES)))
            score = jnp.sum(a, axis=1, keepdims=True)
            acc = jnp.where(lane == SUBLANES * j + s, score, acc)
        p_diag.append(acc)
    p_diag = jnp.where(mask_d > 0.0, jnp.concatenate(p_diag, axis=0), 0.0)
    p_diag = p_diag[:, :CHUNK].astype(BF16)
    return dict(s_off=s_off, o_int=o_int, p_diag=p_diag, v_bf=v_bf), st_new


def _hgrn_mix(a, mask_p):
    v_bf = a["v_bf"]
    v_stack = jnp.concatenate([v_bf[:SUBLANES * _group_blocks(i), :] for i in STACK_ORDER], axis=0)
    p_off = (a["s_off"] * mask_p).astype(BF16)
    o = jnp.dot(p_off, v_stack, preferred_element_type=F32)
    o = o + jnp.dot(a["p_diag"], v_bf, preferred_element_type=F32)
    return o + a["o_int"]


def _hgrn_out(o, gr, norm_g):
    return _rms(o, norm_g) * _silu(gr)


def _score_masks():
    col_blk = []
    for i in STACK_ORDER:
        col_blk += [i] * (SUBLANES * i) + [-1] * (SUBLANES * (_group_blocks(i) - i))
    col_blk = jnp.asarray(col_blk, jnp.int32)[None, :]
    t = jnp.arange(CHUNK, dtype=jnp.int32)[:, None]
    mask_p = (t // SUBLANES == col_blk).astype(F32)
    c = jnp.arange(LANES, dtype=jnp.int32)[None, :]
    mask_d = ((c // SUBLANES == t // SUBLANES) & (c % SUBLANES <= t % SUBLANES)).astype(F32)
    return mask_p, mask_d


def _pool_features(pbuf_ref, t0):
    t_glob = t0 + lax.broadcasted_iota(jnp.int32, (SUBTILE, LANES), 0)
    lane = lax.broadcasted_iota(jnp.int32, (SUBTILE, LANES), 1)
    first_group = lane < POOL_CH

    def window_sum(col, lo, hi):
        acc = None
        for d in range(lo, hi):
            v = pbuf_ref[MAX_WIN - d:MAX_WIN - d + SUBTILE, col:col + LANES]
            acc = v if acc is None else acc + v
        return acc

    parts = []
    for half in range(POOL_WIDTH // LANES):
        col = half * LANES
        w_a, w_b = POOL_WINDOWS[2 * half], POOL_WINDOWS[2 * half + 1]
        sum_a = window_sum(col, 0, w_a)
        sum_b = sum_a + window_sum(col, w_a, w_b)
        cnt_a = jnp.minimum(t_glob + 1, w_a).astype(F32)
        cnt_b = jnp.minimum(t_glob + 1, w_b).astype(F32)
        mean = jnp.where(first_group, sum_a / cnt_a, sum_b / cnt_b)
        parts.append(mean - pbuf_ref[MAX_WIN:MAX_WIN + SUBTILE, col:col + LANES])
    return jnp.concatenate(parts, axis=1)


def _mixer_kernel(x_ref, xn_ref, g_ref, win_hbm, lbp_ref, ng_ref, mask_p_ref, mask_d_ref,
                  pw_ref, psc_ref, wout_hbm, o_ref,
                  win_ref, wout_ref, win_stage, wout_stage, sems,
                  u_refs, hn_refs, y_refs, lk_refs, st_ref, pbuf_ref, *, ts, layer):
    s_idx = pl.program_id(1)
    n_sub = ts // SUBTILE
    chunks = SUBTILE // CHUNK
    piece_cols = [slice(lo, min(lo + PROJ_PIECE, IN_WIDTH)) for lo in range(0, IN_WIDTH, PROJ_PIECE)]
    n_pieces = len(piece_cols)

    @pl.when((pl.program_id(0) == 0) & (s_idx == 0))
    def _():
        def store_in(rows, w):
            win_ref[rows, 0:POOL_COL] = w[:, POOL_WIDTH:]
            win_ref[rows, POOL_COL:IN_WIDTH] = w[:, :POOL_WIDTH]

        def store_out(rows, w):
            wout_ref[rows, :] = w

        _load_weight_bf16(win_hbm.at[layer], win_stage, sems[0], store_in)
        _load_weight_bf16(wout_hbm.at[layer], wout_stage, sems[1], store_out)

    @pl.when(s_idx == 0)
    def _():
        st_ref[...] = jnp.zeros(st_ref.shape, F32)
        pbuf_ref[0:MAX_WIN, :] = jnp.zeros((MAX_WIN, POOL_WIDTH), F32)

    def norm_rows(sub, hn_ref, may_cross):
        rows = pl.ds(pl.multiple_of(jnp.minimum(sub, n_sub - 1) * SUBTILE, SUBTILE), SUBTILE)
        xs = x_ref[rows, :]
        if may_cross:
            xs = jnp.where(sub >= n_sub, xn_ref[...], xs)
        hn_ref[...] = _rms(xs, g_ref[...]).astype(BF16)

    def in_proj_piece(hn_ref, u_ref, k):
        cols = piece_cols[k]
        u_ref[:, cols] = jnp.dot(hn_ref[...], win_ref[:, cols], preferred_element_type=F32)

    def col(base, h):
        return slice(base + h * HEAD_DIM, base + (h + 1) * HEAD_DIM)


    chunk_rows = [slice(c * CHUNK, (c + 1) * CHUNK) for c in range(chunks)]
    items = [(c, h) for c in range(chunks) for h in range(HEADS)]

    def step(sub, half, st, filler):
        u, u_next = u_refs[half], u_refs[1 - half]
        hn_ref, y_ref, lk_ref = hn_refs[half], y_refs[half], lk_refs[half]
        filler = list(filler)
        per_item = -(-len(filler) // len(items))
        norm_rows(sub + 1, hn_ref, may_cross=(half == 1))
        n_slots = len(items) + 1

        def emit_pieces(slot):
            for k in range(n_pieces):
                if k * n_slots // n_pieces == slot:
                    in_proj_piece(hn_ref, u_next, k)

        def pool_mixer():
            pbuf_ref[MAX_WIN:MAX_WIN + SUBTILE, :] = u[:, POOL_COL:POOL_COL + POOL_WIDTH]
            feats = _pool_features(pbuf_ref, s_idx * ts + sub * SUBTILE).astype(BF16)
            y_pool = jnp.dot(feats, pw_ref[...], preferred_element_type=F32) * psc_ref[...]
            y_ref[:, 0:POOL_WIDTH] = y_pool.astype(BF16)
            pbuf_ref[0:MAX_WIN, :] = pbuf_ref[SUBTILE:SUBTILE + MAX_WIN, :]

        emit_pieces(0)

        stage1 = []
        for slot, (c, h) in enumerate(items, start=1):
            if slot == POOL_SLOT:
                pool_mixer()
            rows = chunk_rows[c]
            emit_pieces(slot)
            for thunk in filler[:per_item]:
                thunk()
            del filler[:per_item]
            a, st[h] = _hgrn_scores(
                u[rows, col(Q_COL, h)], u[rows, col(F_COL, h)], u[rows, col(I_COL, h)],
                lbp_ref[h, 0:1, :], lbp_ref[h, 1:2, :], lbp_ref[h, 2:3, :],
                st[h], mask_d_ref[...], lk_ref.at[c, h])
            a["gate"] = u[rows, col(G_COL, h)]
            stage1.append(a)

        stage2 = {}

        def mix(i):
            stage2[i] = _hgrn_mix(stage1[i], mask_p_ref[...])

        def finish(i):
            c, h = items[i]
            y = _hgrn_out(stage2.pop(i), stage1[i]["gate"], ng_ref[h])
            y_ref[chunk_rows[c], col(POOL_WIDTH, h)] = y.astype(BF16)

        def out_proj():
            rows = pl.ds(pl.multiple_of(sub * SUBTILE, SUBTILE), SUBTILE)
            o_ref[rows, :] = x_ref[rows, :] + jnp.dot(y_ref[...], wout_ref[...],
                                                      preferred_element_type=F32)

        tail = []
        for i in range(len(items) + STAGE_LAG):
            if i < len(items):
                tail.append(functools.partial(mix, i))
            if i >= STAGE_LAG:
                tail.append(functools.partial(finish, i - STAGE_LAG))
        tail.append(out_proj)
        return tail

    @pl.when(s_idx == 0)
    def _():
        norm_rows(0, hn_refs[1], may_cross=False)
        for k in range(n_pieces):
            in_proj_piece(hn_refs[1], u_refs[0], k)

    def body(pair, carry):
        st = [st_ref[h] for h in range(HEADS)]
        tail = step(2 * pair, 0, st, [])
        tail = step(2 * pair + 1, 1, st, tail)
        for thunk in tail:
            thunk()
        for h in range(HEADS):
            st_ref[h] = st[h]
        return carry

    lax.fori_loop(0, n_sub // 2, body, 0)


def _mixer(x3, g, w_in, lb_params, norm_g, pool_w_bd, pool_scale, w_out, ts, layer):
    b, s, d = x3.shape
    mask_p, mask_d = _score_masks()
    n_sub = ts // SUBTILE
    last_sub = s // SUBTILE - 1
    const2 = lambda shape: pl.BlockSpec(shape, lambda bi, si: (0, 0))
    const3 = lambda shape: pl.BlockSpec(shape, lambda bi, si: (0, 0, 0))
    return pl.pallas_call(
        functools.partial(_mixer_kernel, ts=ts, layer=layer),
        grid=(b, s // ts),
        in_specs=[
            pl.BlockSpec((None, ts, d), lambda bi, si: (bi, si, 0)),
            pl.BlockSpec((None, SUBTILE, d),
                         lambda bi, si: (bi, jnp.minimum((si + 1) * n_sub, last_sub), 0)),
            const2((1, d)),
            pl.BlockSpec(memory_space=pl.ANY),
            const3((HEADS, 8, HEAD_DIM)),
            const3((HEADS, 1, HEAD_DIM)),
            const2((CHUNK, STACK_PAD)),
            const2((CHUNK, LANES)),
            const2((POOL_WIDTH, POOL_WIDTH)),
            const2((1, POOL_WIDTH)),
            pl.BlockSpec(memory_space=pl.ANY),
        ],
        out_specs=pl.BlockSpec((None, ts, d), lambda bi, si: (bi, si, 0)),
        out_shape=jax.ShapeDtypeStruct((b, s, d), F32),
        scratch_shapes=[
            pltpu.VMEM((d, IN_WIDTH), BF16),
            pltpu.VMEM((d, d), BF16),
            pltpu.VMEM((2, WEIGHT_STAGE_ROWS, IN_WIDTH), F32),
            pltpu.VMEM((2, WEIGHT_STAGE_ROWS, d), F32),
            [pltpu.SemaphoreType.DMA((2,))] * 2,
            [pltpu.VMEM((SUBTILE, IN_WIDTH), F32)] * 2,
            [pltpu.VMEM((SUBTILE, d), BF16)] * 2,
            [pltpu.VMEM((SUBTILE, d), BF16)] * 2,
            [pltpu.VMEM((SUBTILE // CHUNK, HEADS, CHUNK, LANES), F32)] * 2,
            pltpu.VMEM((HEADS, HEAD_DIM, HEAD_DIM), F32),
            pltpu.VMEM((SUBTILE + MAX_WIN, POOL_WIDTH), F32),
        ],
        compiler_params=pltpu.CompilerParams(
            dimension_semantics=("arbitrary", "arbitrary"), vmem_limit_bytes=VMEM_LIMIT),
        name="mixer",
    )(x3, x3, g, w_in, lb_params, norm_g, mask_p, mask_d, pool_w_bd, pool_scale, w_out)


def _mlp_kernel(x_ref, g_ref, wu_hbm, wd_hbm, fg_ref, o_ref,
                wu_ref, wd_ref, wu_stage, wd_stage, sems, *, ff_tile, final_norm, layer):
    @pl.when(pl.program_id(0) == 0)
    def _():
        def store_up(rows, w):
            wu_ref[rows, :] = w

        def store_down(rows, w):
            wd_ref[rows, :] = w

        _load_weight_bf16(wu_hbm.at[layer], wu_stage, sems[0], store_up)
        _load_weight_bf16(wd_hbm.at[layer], wd_stage, sems[1], store_down)

    x = x_ref[...]
    h = _rms(x, g_ref[...]).astype(BF16)
    acc = x
    for n in range(D_FF // ff_tile):
        cols = slice(n * ff_tile, (n + 1) * ff_tile)
        a = jnp.dot(h, wu_ref[:, cols], preferred_element_type=F32)
        a = jnp.square(jnp.maximum(a, 0.0)).astype(BF16)
        acc = acc + jnp.dot(a, wd_ref[cols, :], preferred_element_type=F32)
    o_ref[...] = _rms(acc, fg_ref[...]) if final_norm else acc


def _mlp(x2, g, w_up, w_down, final_g, tm, layer, final_norm, ff_tile=1024):
    t = x2.shape[0]
    return pl.pallas_call(
        functools.partial(_mlp_kernel, ff_tile=ff_tile, final_norm=final_norm, layer=layer),
        grid=(t // tm,),
        in_specs=[
            pl.BlockSpec((tm, D_MODEL), lambda i: (i, 0)),
            pl.BlockSpec((1, D_MODEL), lambda i: (0, 0)),
            pl.BlockSpec(memory_space=pl.ANY),
            pl.BlockSpec(memory_space=pl.ANY),
            pl.BlockSpec((1, D_MODEL), lambda i: (0, 0)),
        ],
        out_specs=pl.BlockSpec((tm, D_MODEL), lambda i: (i, 0)),
        out_shape=jax.ShapeDtypeStruct((t, D_MODEL), F32),
        scratch_shapes=[
            pltpu.VMEM((D_MODEL, D_FF), BF16),
            pltpu.VMEM((D_FF, D_MODEL), BF16),
            pltpu.VMEM((2, WEIGHT_STAGE_ROWS, D_FF), F32),
            pltpu.VMEM((2, 4 * WEIGHT_STAGE_ROWS, D_MODEL), F32),
            [pltpu.SemaphoreType.DMA((2,))] * 2,
        ],
        compiler_params=pltpu.CompilerParams(
            dimension_semantics=("arbitrary",), vmem_limit_bytes=VMEM_LIMIT),
        name="mlp",
    )(x2, g, w_up, w_down, final_g)


def _block_diag(w):
    g, c, _ = w.shape
    eye = jnp.eye(g, dtype=w.dtype)
    return (eye[:, None, :, None] * w[:, :, None, :]).reshape(g * c, g * c)


def kernel(x, norm_mix_g, w_in, pool_w, pool_scale, hgrn_lb_logits, hgrn_norm_g, w_out,
           norm_mlp_g, w_up, w_down, final_norm_g):
    b, s, d = x.shape
    depth = w_in.shape[0]
    t = b * s
    tm = 512
    ts = 2048

    lb_cum = jnp.cumsum(jax.nn.softmax(hgrn_lb_logits.astype(F32), axis=0), axis=0)
    lower = lb_cum - lb_cum[0:1]
    lbp = jnp.stack([lower, 1.0 - lower, jnp.log1p(-lower)], axis=1)
    lbp = jnp.pad(lbp, ((0, 0), (0, 5), (0, 0)))
    lbp = lbp.reshape(depth, 8, HEADS, HEAD_DIM).transpose(0, 2, 1, 3)

    for l in range(depth):
        x = _mixer(x, norm_mix_g[l][None, :], w_in, lbp[l],
                   hgrn_norm_g[l].reshape(HEADS, 1, HEAD_DIM),
                   _block_diag(pool_w[l]).astype(BF16), pool_scale[l][None, :],
                   w_out, ts, layer=l)
        x = _mlp(x.reshape(t, d), norm_mlp_g[l][None, :], w_up, w_down,
                 final_norm_g[None, :], tm, layer=l,
                 final_norm=(l == depth - 1)).reshape(b, s, d)
    return x
```

```python
import functools

import jax
import jax.numpy as jnp
from jax import lax
from jax.experimental import pallas as pl
from jax.experimental.pallas import tpu as pltpu

D_MODEL = 1024
POOL_WINDOWS = (2, 4, 8, 16)
POOL_CH = 64
POOL_WIDTH = len(POOL_WINDOWS) * POOL_CH
MAX_WIN = max(POOL_WINDOWS)
HEAD_DIM = 128
HGRN_WIDTH = D_MODEL - POOL_WIDTH
HEADS = HGRN_WIDTH // HEAD_DIM
IN_WIDTH = POOL_WIDTH + 4 * HGRN_WIDTH
D_FF = 4 * D_MODEL
RMS_EPS = 1e-5
LOG2_E = 1.4426950408889634

LANES = 128
SUBLANES = 8
CHUNK = 64
NBLK = CHUNK // SUBLANES
BF16_TILE_BLOCKS = 2
STACK_ORDER = tuple(range(1, NBLK))
SUBTILE = 2 * CHUNK
PROJ_PIECE = 256
WEIGHT_STAGE_ROWS = 128
STAGE_LAG = 3
POOL_SLOT = 7
Q_COL, F_COL, I_COL, G_COL, POOL_COL = (j * HGRN_WIDTH for j in range(5))
VMEM_LIMIT = 56 * 1024 * 1024

BF16 = jnp.bfloat16
F32 = jnp.float32


def _group_blocks(i):
    return -(-i // BF16_TILE_BLOCKS) * BF16_TILE_BLOCKS


STACK_PAD = SUBLANES * sum(_group_blocks(i) for i in STACK_ORDER)


def _rms(x, g):
    ms = jnp.mean(x * x, axis=-1, keepdims=True)
    return x * lax.rsqrt(ms + RMS_EPS) * g


def _neg_abs(x):
    return jnp.minimum(x, -x)


def _silu(x):
    h = 0.5 * x
    return h + h * jnp.tanh(h)


def _bcast_row(x, r):
    return jnp.broadcast_to(x[r:r + 1, :], x.shape)


def _load_weight_bf16(w_hbm, stage_ref, sem, store):
    rows_per = stage_ref.shape[1]
    n = w_hbm.shape[0] // rows_per

    def copy(i):
        return pltpu.make_async_copy(w_hbm.at[pl.ds(i * rows_per, rows_per), :],
                                     stage_ref.at[i % 2], sem.at[i % 2])

    copy(0).start()
    for i in range(n):
        if i + 1 < n:
            copy(i + 1).start()
        copy(i).wait()
        store(slice(i * rows_per, (i + 1) * rows_per), stage_ref[i % 2].astype(BF16))


def _hgrn_scores(qr, z, v, lb, one_m_lb, log_one_m_lb, st, mask_d, lk_ref):
    e = jnp.exp(_neg_abs(z))
    d = 1.0 + e
    r = 1.0 / d
    sig_p = jnp.where(z >= 0.0, r, e * r)
    log_sig = jnp.minimum(z, 0.0) - jnp.log(d)
    log_gate = log_one_m_lb + log_sig
    f = lb + one_m_lb * sig_p
    logf2 = jnp.maximum(jnp.log(f), log_gate) * LOG2_E
    log2_kk = (log_gate - z) * LOG2_E
    q = _silu(qr)

    row = lax.broadcasted_iota(jnp.int32, (SUBLANES, LANES), 0)
    lane = lax.broadcasted_iota(jnp.int32, (SUBLANES, LANES), 1)
    blk = lambda a, j: a[SUBLANES * j:SUBLANES * (j + 1), :]

    cb = []
    for j in range(NBLK):
        c = blk(logf2, j)
        for sh in (1, 2, 4):
            c = c + jnp.where(row >= sh, pltpu.roll(c, sh, 0), 0.0)
        cb.append(c)
    tot = [_bcast_row(c, SUBLANES - 1) for c in cb]
    carry = [jnp.zeros((SUBLANES, LANES), F32)]
    for j in range(NBLK):
        carry.append(carry[-1] + tot[j])

    qb = [blk(q, j) for j in range(NBLK)]
    lk = [blk(log2_kk, j) - cb[j] for j in range(NBLK)]
    for j in range(NBLK):
        lk_ref[SUBLANES * j:SUBLANES * (j + 1), :] = lk[j]

    q_blk = jnp.concatenate([qb[j] * jnp.exp2(cb[j]) for j in range(NBLK)], axis=0)
    q_chunk = jnp.concatenate([qb[j] * jnp.exp2(cb[j] + carry[j]) for j in range(NBLK)], axis=0)
    k_hat = [jnp.exp2(lk[j] + tot[j]) for j in range(NBLK)]
    e_tot = [jnp.exp2(tot[j]) for j in range(NBLK)]
    cur = {}
    k_groups = {}
    for i in range(1, NBLK + 1):
        for j in range(i - 1):
            cur[j] = cur[j] * e_tot[i - 1]
        cur[i - 1] = k_hat[i - 1]
        if i < NBLK:
            k_groups[i] = [cur[j] for j in range(i)] + [k_hat[j] for j in range(i, _group_blocks(i))]
    k_chunk = jnp.concatenate([cur[j] for j in range(NBLK)], axis=0)
    v_bf = v.astype(BF16)
    k_stack = jnp.concatenate([p for i in STACK_ORDER for p in k_groups[i]], axis=0).astype(BF16)

    nt = (((1,), (1,)), ((), ()))
    tn = (((0,), (0,)), ((), ()))
    s_off = lax.dot_general(q_blk.astype(BF16), k_stack, nt, preferred_element_type=F32)
    o_int = lax.dot_general(q_chunk.astype(BF16), st.astype(BF16), nt,
                            preferred_element_type=F32)
    kv = lax.dot_general(v_bf, k_chunk.astype(BF16), tn, preferred_element_type=F32)
    st_new = st * jnp.exp2(carry[NBLK][0:1, :]) + kv

    p_diag = []
    for j in range(NBLK):
        acc = jnp.zeros((SUBLANES, LANES), F32)
        for s in range(SUBLANES):
            lk_s = lk_ref[SUBLANES * j + s:SUBLANES * j + s + 1, :]
            a = qb[j] * jnp.exp2(cb[j] + jnp.broadcast_to(lk_s, (SUBLANES, LANES)))
            score = jnp.sum(a, axis=1, keepdims=True)
            acc = jnp.where(lane == SUBLANES * j + s, score, acc)
        p_diag.append(acc)
    p_diag = jnp.where(mask_d > 0.0, jnp.concatenate(p_diag, axis=0), 0.0)
    p_diag = p_diag[:, :CHUNK].astype(BF16)
    return dict(s_off=s_off, o_int=o_int, p_diag=p_diag, v_bf=v_bf), st_new


def _hgrn_mix(a, mask_p):
    v_bf = a["v_bf"]
    v_stack = jnp.concatenate([v_bf[:SUBLANES * _group_blocks(i), :] for i in STACK_ORDER], axis=0)
    p_off = (a["s_off"] * mask_p).astype(BF16)
    o = jnp.dot(p_off, v_stack, preferred_element_type=F32)
    o = o + jnp.dot(a["p_diag"], v_bf, preferred_element_type=F32)
    return o + a["o_int"]


def _hgrn_out(o, gr, norm_g):
    return _rms(o, norm_g) * _silu(gr)


def _score_masks():
    col_blk = []
    for i in STACK_ORDER:
        col_blk += [i] * (SUBLANES * i) + [-1] * (SUBLANES * (_group_blocks(i) - i))
    col_blk = jnp.asarray(col_blk, jnp.int32)[None, :]
    t = jnp.arange(CHUNK, dtype=jnp.int32)[:, None]
    mask_p = (t // SUBLANES == col_blk).astype(F32)
    c = jnp.arange(LANES, dtype=jnp.int32)[None, :]
    mask_d = ((c // SUBLANES == t // SUBLANES) & (c % SUBLANES <= t % SUBLANES)).astype(F32)
    return mask_p, mask_d


def _pool_features(pbuf_ref, t0):
    t_glob = t0 + lax.broadcasted_iota(jnp.int32, (SUBTILE, LANES), 0)
    lane = lax.broadcasted_iota(jnp.int32, (SUBTILE, LANES), 1)
    first_group = lane < POOL_CH

    def window_sum(col, lo, hi):
        acc = None
        for d in range(lo, hi):
            v = pbuf_ref[MAX_WIN - d:MAX_WIN - d + SUBTILE, col:col + LANES]
            acc = v if acc is None else acc + v
        return acc

    parts = []
    for half in range(POOL_WIDTH // LANES):
        col = half * LANES
        w_a, w_b = POOL_WINDOWS[2 * half], POOL_WINDOWS[2 * half + 1]
        sum_a = window_sum(col, 0, w_a)
        sum_b = sum_a + window_sum(col, w_a, w_b)
        cnt_a = jnp.minimum(t_glob + 1, w_a).astype(F32)
        cnt_b = jnp.minimum(t_glob + 1, w_b).astype(F32)
        mean = jnp.where(first_group, sum_a / cnt_a, sum_b / cnt_b)
        parts.append(mean - pbuf_ref[MAX_WIN:MAX_WIN + SUBTILE, col:col + LANES])
    return jnp.concatenate(parts, axis=1)


def _mixer_kernel(x_ref, xn_ref, g_ref, win_hbm, lbp_ref, ng_ref, mask_p_ref, mask_d_ref,
                  pw_ref, psc_ref, wout_hbm, o_ref,
                  win_ref, wout_ref, win_stage, wout_stage, sems,
                  u_refs, hn_refs, y_refs, lk_refs, st_ref, pbuf_ref, *, ts, layer):
    s_idx = pl.program_id(1)
    n_sub = ts // SUBTILE
    chunks = SUBTILE // CHUNK
    piece_cols = [slice(lo, min(lo + PROJ_PIECE, IN_WIDTH)) for lo in range(0, IN_WIDTH, PROJ_PIECE)]
    n_pieces = len(piece_cols)

    @pl.when((pl.program_id(0) == 0) & (s_idx == 0))
    def _():
        def store_in(rows, w):
            win_ref[rows, 0:POOL_COL] = w[:, POOL_WIDTH:]
            win_ref[rows, POOL_COL:IN_WIDTH] = w[:, :POOL_WIDTH]

        def store_out(rows, w):
            wout_ref[rows, :] = w

        _load_weight_bf16(win_hbm.at[layer], win_stage, sems[0], store_in)
        _load_weight_bf16(wout_hbm.at[layer], wout_stage, sems[1], store_out)

    @pl.when(s_idx == 0)
    def _():
        st_ref[...] = jnp.zeros(st_ref.shape, F32)
        pbuf_ref[0:MAX_WIN, :] = jnp.zeros((MAX_WIN, POOL_WIDTH), F32)

    def norm_rows(sub, hn_ref, may_cross):
        rows = pl.ds(pl.multiple_of(jnp.minimum(sub, n_sub - 1) * SUBTILE, SUBTILE), SUBTILE)
        xs = x_ref[rows, :]
        if may_cross:
            xs = jnp.where(sub >= n_sub, xn_ref[...], xs)
        hn_ref[...] = _rms(xs, g_ref[...]).astype(BF16)

    def in_proj_piece(hn_ref, u_ref, k):
        cols = piece_cols[k]
        u_ref[:, cols] = jnp.dot(hn_ref[...], win_ref[:, cols], preferred_element_type=F32)

    def col(base, h):
        return slice(base + h * HEAD_DIM, base + (h + 1) * HEAD_DIM)


    chunk_rows = [slice(c * CHUNK, (c + 1) * CHUNK) for c in range(chunks)]
    items = [(c, h) for c in range(chunks) for h in range(HEADS)]

    def step(sub, half, st, filler):
        u, u_next = u_refs[half], u_refs[1 - half]
        hn_ref, y_ref, lk_ref = hn_refs[half], y_refs[half], lk_refs[half]
        filler = list(filler)
        per_item = -(-len(filler) // len(items))
        norm_rows(sub + 1, hn_ref, may_cross=(half == 1))
        n_slots = len(items) + 1

        def emit_pieces(slot):
            for k in range(n_pieces):
                if k * n_slots // n_pieces == slot:
                    in_proj_piece(hn_ref, u_next, k)

        def pool_mixer():
            pbuf_ref[MAX_WIN:MAX_WIN + SUBTILE, :] = u[:, POOL_COL:POOL_COL + POOL_WIDTH]
            feats = _pool_features(pbuf_ref, s_idx * ts + sub * SUBTILE).astype(BF16)
            y_pool = jnp.dot(feats, pw_ref[...], preferred_element_type=F32) * psc_ref[...]
            y_ref[:, 0:POOL_WIDTH] = y_pool.astype(BF16)
            pbuf_ref[0:MAX_WIN, :] = pbuf_ref[SUBTILE:SUBTILE + MAX_WIN, :]

        emit_pieces(0)

        stage1 = []
        for slot, (c, h) in enumerate(items, start=1):
            if slot == POOL_SLOT:
                pool_mixer()
            rows = chunk_rows[c]
            emit_pieces(slot)
            for thunk in filler[:per_item]:
                thunk()
            del filler[:per_item]
            a, st[h] = _hgrn_scores(
                u[rows, col(Q_COL, h)], u[rows, col(F_COL, h)], u[rows, col(I_COL, h)],
                lbp_ref[h, 0:1, :], lbp_ref[h, 1:2, :], lbp_ref[h, 2:3, :],
                st[h], mask_d_ref[...], lk_ref.at[c, h])
            a["gate"] = u[rows, col(G_COL, h)]
            stage1.append(a)

        stage2 = {}

        def mix(i):
            stage2[i] = _hgrn_mix(stage1[i], mask_p_ref[...])

        def finish(i):
            c, h = items[i]
            y = _hgrn_out(stage2.pop(i), stage1[i]["gate"], ng_ref[h])
            y_ref[chunk_rows[c], col(POOL_WIDTH, h)] = y.astype(BF16)

        def out_proj():
            rows = pl.ds(pl.multiple_of(sub * SUBTILE, SUBTILE), SUBTILE)
            o_ref[rows, :] = x_ref[rows, :] + jnp.dot(y_ref[...], wout_ref[...],
                                                      preferred_element_type=F32)

        tail = []
        for i in range(len(items) + STAGE_LAG):
            if i < len(items):
                tail.append(functools.partial(mix, i))
            if i >= STAGE_LAG:
                tail.append(functools.partial(finish, i - STAGE_LAG))
        tail.append(out_proj)
        return tail

    @pl.when(s_idx == 0)
    def _():
        norm_rows(0, hn_refs[1], may_cross=False)
        for k in range(n_pieces):
            in_proj_piece(hn_refs[1], u_refs[0], k)

    def body(pair, carry):
        st = [st_ref[h] for h in range(HEADS)]
        tail = step(2 * pair, 0, st, [])
        tail = step(2 * pair + 1, 1, st, tail)
        for thunk in tail:
            thunk()
        for h in range(HEADS):
            st_ref[h] = st[h]
        return carry

    lax.fori_loop(0, n_sub // 2, body, 0)


def _mixer(x3, g, w_in, lb_params, norm_g, pool_w_bd, pool_scale, w_out, ts, layer):
    b, s, d = x3.shape
    mask_p, mask_d = _score_masks()
    n_sub = ts // SUBTILE
    last_sub = s // SUBTILE - 1
    const2 = lambda shape: pl.BlockSpec(shape, lambda bi, si: (0, 0))
    const3 = lambda shape: pl.BlockSpec(shape, lambda bi, si: (0, 0, 0))
    return pl.pallas_call(
        functools.partial(_mixer_kernel, ts=ts, layer=layer),
        grid=(b, s // ts),
        in_specs=[
            pl.BlockSpec((None, ts, d), lambda bi, si: (bi, si, 0)),
            pl.BlockSpec((None, SUBTILE, d),
                         lambda bi, si: (bi, jnp.minimum((si + 1) * n_sub, last_sub), 0)),
            const2((1, d)),
            pl.BlockSpec(memory_space=pl.ANY),
            const3((HEADS, 8, HEAD_DIM)),
            const3((HEADS, 1, HEAD_DIM)),
            const2((CHUNK, STACK_PAD)),
            const2((CHUNK, LANES)),
            const2((POOL_WIDTH, POOL_WIDTH)),
            const2((1, POOL_WIDTH)),
            pl.BlockSpec(memory_space=pl.ANY),
        ],
        out_specs=pl.BlockSpec((None, ts, d), lambda bi, si: (bi, si, 0)),
        out_shape=jax.ShapeDtypeStruct((b, s, d), F32),
        scratch_shapes=[
            pltpu.VMEM((d, IN_WIDTH), BF16),
            pltpu.VMEM((d, d), BF16),
            pltpu.VMEM((2, WEIGHT_STAGE_ROWS, IN_WIDTH), F32),
            pltpu.VMEM((2, WEIGHT_STAGE_ROWS, d), F32),
            [pltpu.SemaphoreType.DMA((2,))] * 2,
            [pltpu.VMEM((SUBTILE, IN_WIDTH), F32)] * 2,
            [pltpu.VMEM((SUBTILE, d), BF16)] * 2,
            [pltpu.VMEM((SUBTILE, d), BF16)] * 2,
            [pltpu.VMEM((SUBTILE // CHUNK, HEADS, CHUNK, LANES), F32)] * 2,
            pltpu.VMEM((HEADS, HEAD_DIM, HEAD_DIM), F32),
            pltpu.VMEM((SUBTILE + MAX_WIN, POOL_WIDTH), F32),
        ],
        compiler_params=pltpu.CompilerParams(
            dimension_semantics=("arbitrary", "arbitrary"), vmem_limit_bytes=VMEM_LIMIT),
        name="mixer",
    )(x3, x3, g, w_in, lb_params, norm_g, mask_p, mask_d, pool_w_bd, pool_scale, w_out)


def _mlp_kernel(x_ref, g_ref, wu_hbm, wd_hbm, fg_ref, o_ref,
                wu_ref, wd_ref, wu_stage, wd_stage, sems, *, ff_tile, final_norm, layer):
    @pl.when(pl.program_id(0) == 0)
    def _():
        def store_up(rows, w):
            wu_ref[rows, :] = w

        def store_down(rows, w):
            wd_ref[rows, :] = w

        _load_weight_bf16(wu_hbm.at[layer], wu_stage, sems[0], store_up)
        _load_weight_bf16(wd_hbm.at[layer], wd_stage, sems[1], store_down)

    x = x_ref[...]
    h = _rms(x, g_ref[...]).astype(BF16)
    acc = x
    for n in range(D_FF // ff_tile):
        cols = slice(n * ff_tile, (n + 1) * ff_tile)
        a = jnp.dot(h, wu_ref[:, cols], preferred_element_type=F32)
        a = jnp.square(jnp.maximum(a, 0.0)).astype(BF16)
        acc = acc + jnp.dot(a, wd_ref[cols, :], preferred_element_type=F32)
    o_ref[...] = _rms(acc, fg_ref[...]) if final_norm else acc


def _mlp(x2, g, w_up, w_down, final_g, tm, layer, final_norm, ff_tile=1024):
    t = x2.shape[0]
    return pl.pallas_call(
        functools.partial(_mlp_kernel, ff_tile=ff_tile, final_norm=final_norm, layer=layer),
        grid=(t // tm,),
        in_specs=[
            pl.BlockSpec((tm, D_MODEL), lambda i: (i, 0)),
            pl.BlockSpec((1, D_MODEL), lambda i: (0, 0)),
            pl.BlockSpec(memory_space=pl.ANY),
            pl.BlockSpec(memory_space=pl.ANY),
            pl.BlockSpec((1, D_MODEL), lambda i: (0, 0)),
        ],
        out_specs=pl.BlockSpec((tm, D_MODEL), lambda i: (i, 0)),
        out_shape=jax.ShapeDtypeStruct((t, D_MODEL), F32),
        scratch_shapes=[
            pltpu.VMEM((D_MODEL, D_FF), BF16),
            pltpu.VMEM((D_FF, D_MODEL), BF16),
            pltpu.VMEM((2, WEIGHT_STAGE_ROWS, D_FF), F32),
            pltpu.VMEM((2, 4 * WEIGHT_STAGE_ROWS, D_MODEL), F32),
            [pltpu.SemaphoreType.DMA((2,))] * 2,
        ],
        compiler_params=pltpu.CompilerParams(
            dimension_semantics=("arbitrary",), vmem_limit_bytes=VMEM_LIMIT),
        name="mlp",
    )(x2, g, w_up, w_down, final_g)


def _block_diag(w):
    g, c, _ = w.shape
    eye = jnp.eye(g, dtype=w.dtype)
    return (eye[:, None, :, None] * w[:, :, None, :]).reshape(g * c, g * c)


def kernel(x, norm_mix_g, w_in, pool_w, pool_scale, hgrn_lb_logits, hgrn_norm_g, w_out,
           norm_mlp_g, w_up, w_down, final_norm_g):
    b, s, d = x.shape
    depth = w_in.shape[0]
    t = b * s
    tm = 1024
    ts = 2048

    lb_cum = jnp.cumsum(jax.nn.softmax(hgrn_lb_logits.astype(F32), axis=0), axis=0)
    lower = lb_cum - lb_cum[0:1]
    lbp = jnp.stack([lower, 1.0 - lower, jnp.log1p(-lower)], axis=1)
    lbp = jnp.pad(lbp, ((0, 0), (0, 5), (0, 0)))
    lbp = lbp.reshape(depth, 8, HEADS, HEAD_DIM).transpose(0, 2, 1, 3)

    for l in range(depth):
        x = _mixer(x, norm_mix_g[l][None, :], w_in, lbp[l],
                   hgrn_norm_g[l].reshape(HEADS, 1, HEAD_DIM),
                   _block_diag(pool_w[l]).astype(BF16), pool_scale[l][None, :],
                   w_out, ts, layer=l)
        x = _mlp(x.reshape(t, d), norm_mlp_g[l][None, :], w_up, w_down,
                 final_norm_g[None, :], tm, layer=l,
                 final_norm=(l == depth - 1)).reshape(b, s, d)
    return x
```

```python
import functools

import jax
import jax.numpy as jnp
from jax import lax
from jax.experimental import pallas as pl
from jax.experimental.pallas import tpu as pltpu

D_MODEL = 1024
POOL_WINDOWS = (2, 4, 8, 16)
POOL_CH = 64
POOL_WIDTH = len(POOL_WINDOWS) * POOL_CH
MAX_WIN = max(POOL_WINDOWS)
HEAD_DIM = 128
HGRN_WIDTH = D_MODEL - POOL_WIDTH
HEADS = HGRN_WIDTH // HEAD_DIM
IN_WIDTH = POOL_WIDTH + 4 * HGRN_WIDTH
D_FF = 4 * D_MODEL
RMS_EPS = 1e-5
LOG2_E = 1.4426950408889634

LANES = 128
SUBLANES = 8
CHUNK = 64
NBLK = CHUNK // SUBLANES
BF16_TILE_BLOCKS = 2
STACK_ORDER = tuple(range(1, NBLK))
SUBTILE = 2 * CHUNK
PROJ_PIECE = 256
WEIGHT_STAGE_ROWS = 128
STAGE_LAG = 3
POOL_SLOT = 7
Q_COL, F_COL, I_COL, G_COL, POOL_COL = (j * HGRN_WIDTH for j in range(5))
VMEM_LIMIT = 56 * 1024 * 1024

BF16 = jnp.bfloat16
F32 = jnp.float32


def _group_blocks(i):
    return -(-i // BF16_TILE_BLOCKS) * BF16_TILE_BLOCKS


STACK_PAD = SUBLANES * sum(_group_blocks(i) for i in STACK_ORDER)


def _rms(x, g):
    ms = jnp.mean(x * x, axis=-1, keepdims=True)
    return x * lax.rsqrt(ms + RMS_EPS) * g


def _neg_abs(x):
    return jnp.minimum(x, -x)


def _silu(x):
    h = 0.5 * x
    return h + h * jnp.tanh(h)


def _bcast_row(x, r):
    return jnp.broadcast_to(x[r:r + 1, :], x.shape)


def _load_weight_bf16(w_hbm, stage_ref, sem, store):
    rows_per = stage_ref.shape[1]
    n = w_hbm.shape[0] // rows_per

    def copy(i):
        return pltpu.make_async_copy(w_hbm.at[pl.ds(i * rows_per, rows_per), :],
                                     stage_ref.at[i % 2], sem.at[i % 2])

    copy(0).start(priority=0)
    for i in range(n):
        if i + 1 < n:
            copy(i + 1).start(priority=(i + 1) % 2)
        copy(i).wait()
        store(slice(i * rows_per, (i + 1) * rows_per), stage_ref[i % 2].astype(BF16))


def _hgrn_scores(qr, z, v, lb, one_m_lb, log_one_m_lb, st, mask_d, lk_ref):
    e = jnp.exp(_neg_abs(z))
    d = 1.0 + e
    r = 1.0 / d
    sig_p = jnp.where(z >= 0.0, r, e * r)
    log_sig = jnp.minimum(z, 0.0) - jnp.log(d)
    log_gate = log_one_m_lb + log_sig
    f = lb + one_m_lb * sig_p
    logf2 = jnp.maximum(jnp.log(f), log_gate) * LOG2_E
    log2_kk = (log_gate - z) * LOG2_E
    q = _silu(qr)

    row = lax.broadcasted_iota(jnp.int32, (SUBLANES, LANES), 0)
    lane = lax.broadcasted_iota(jnp.int32, (SUBLANES, LANES), 1)
    blk = lambda a, j: a[SUBLANES * j:SUBLANES * (j + 1), :]

    cb = []
    for j in range(NBLK):
        c = blk(logf2, j)
        for sh in (1, 2, 4):
            c = c + jnp.where(row >= sh, pltpu.roll(c, sh, 0), 0.0)
        cb.append(c)
    tot = [_bcast_row(c, SUBLANES - 1) for c in cb]
    carry = [jnp.zeros((SUBLANES, LANES), F32)]
    for j in range(NBLK):
        carry.append(carry[-1] + tot[j])

    qb = [blk(q, j) for j in range(NBLK)]
    lk = [blk(log2_kk, j) - cb[j] for j in range(NBLK)]
    for j in range(NBLK):
        lk_ref[SUBLANES * j:SUBLANES * (j + 1), :] = lk[j]

    q_blk = jnp.concatenate([qb[j] * jnp.exp2(cb[j]) for j in range(NBLK)], axis=0)
    q_chunk = jnp.concatenate([qb[j] * jnp.exp2(cb[j] + carry[j]) for j in range(NBLK)], axis=0)
    k_hat = [jnp.exp2(lk[j] + tot[j]) for j in range(NBLK)]
    e_tot = [jnp.exp2(tot[j]) for j in range(NBLK)]
    cur = {}
    k_groups = {}
    for i in range(1, NBLK + 1):
        for j in range(i - 1):
            cur[j] = cur[j] * e_tot[i - 1]
        cur[i - 1] = k_hat[i - 1]
        if i < NBLK:
            k_groups[i] = [cur[j] for j in range(i)] + [k_hat[j] for j in range(i, _group_blocks(i))]
    k_chunk = jnp.concatenate([cur[j] for j in range(NBLK)], axis=0)
    v_bf = v.astype(BF16)
    k_stack = jnp.concatenate([p for i in STACK_ORDER for p in k_groups[i]], axis=0).astype(BF16)

    nt = (((1,), (1,)), ((), ()))
    tn = (((0,), (0,)), ((), ()))
    s_off = lax.dot_general(q_blk.astype(BF16), k_stack, nt, preferred_element_type=F32)
    o_int = lax.dot_general(q_chunk.astype(BF16), st.astype(BF16), nt,
                            preferred_element_type=F32)
    kv = lax.dot_general(v_bf, k_chunk.astype(BF16), tn, preferred_element_type=F32)
    st_new = st * jnp.exp2(carry[NBLK][0:1, :]) + kv

    p_diag = []
    for j in range(NBLK):
        acc = jnp.zeros((SUBLANES, LANES), F32)
        for s in range(SUBLANES):
            lk_s = lk_ref[SUBLANES * j + s:SUBLANES * j + s + 1, :]
            a = qb[j] * jnp.exp2(cb[j] + jnp.broadcast_to(lk_s, (SUBLANES, LANES)))
            score = jnp.sum(a, axis=1, keepdims=True)
            acc = jnp.where(lane == SUBLANES * j + s, score, acc)
        p_diag.append(acc)
    p_diag = jnp.where(mask_d > 0.0, jnp.concatenate(p_diag, axis=0), 0.0)
    p_diag = p_diag[:, :CHUNK].astype(BF16)
    return dict(s_off=s_off, o_int=o_int, p_diag=p_diag, v_bf=v_bf), st_new


def _hgrn_mix(a, mask_p):
    v_bf = a["v_bf"]
    v_stack = jnp.concatenate([v_bf[:SUBLANES * _group_blocks(i), :] for i in STACK_ORDER], axis=0)
    p_off = (a["s_off"] * mask_p).astype(BF16)
    o = jnp.dot(p_off, v_stack, preferred_element_type=F32)
    o = o + jnp.dot(a["p_diag"], v_bf, preferred_element_type=F32)
    return o + a["o_int"]


def _hgrn_out(o, gr, norm_g):
    return _rms(o, norm_g) * _silu(gr)


def _score_masks():
    col_blk = []
    for i in STACK_ORDER:
        col_blk += [i] * (SUBLANES * i) + [-1] * (SUBLANES * (_group_blocks(i) - i))
    col_blk = jnp.asarray(col_blk, jnp.int32)[None, :]
    t = jnp.arange(CHUNK, dtype=jnp.int32)[:, None]
    mask_p = (t // SUBLANES == col_blk).astype(F32)
    c = jnp.arange(LANES, dtype=jnp.int32)[None, :]
    mask_d = ((c // SUBLANES == t // SUBLANES) & (c % SUBLANES <= t % SUBLANES)).astype(F32)
    return mask_p, mask_d


def _pool_features(pbuf_ref, t0):
    t_glob = t0 + lax.broadcasted_iota(jnp.int32, (SUBTILE, LANES), 0)
    lane = lax.broadcasted_iota(jnp.int32, (SUBTILE, LANES), 1)
    first_group = lane < POOL_CH

    def window_sum(col, lo, hi):
        acc = None
        for d in range(lo, hi):
            v = pbuf_ref[MAX_WIN - d:MAX_WIN - d + SUBTILE, col:col + LANES]
            acc = v if acc is None else acc + v
        return acc

    parts = []
    for half in range(POOL_WIDTH // LANES):
        col = half * LANES
        w_a, w_b = POOL_WINDOWS[2 * half], POOL_WINDOWS[2 * half + 1]
        sum_a = window_sum(col, 0, w_a)
        sum_b = sum_a + window_sum(col, w_a, w_b)
        cnt_a = jnp.minimum(t_glob + 1, w_a).astype(F32)
        cnt_b = jnp.minimum(t_glob + 1, w_b).astype(F32)
        mean = jnp.where(first_group, sum_a / cnt_a, sum_b / cnt_b)
        parts.append(mean - pbuf_ref[MAX_WIN:MAX_WIN + SUBTILE, col:col + LANES])
    return jnp.concatenate(parts, axis=1)


def _mixer_kernel(x_ref, xn_ref, g_ref, win_hbm, lbp_ref, ng_ref, mask_p_ref, mask_d_ref,
                  pw_ref, psc_ref, wout_hbm, o_ref,
                  win_ref, wout_ref, win_stage, wout_stage, sems,
                  u_refs, hn_refs, y_refs, lk_refs, st_ref, pbuf_ref, *, ts, layer):
    s_idx = pl.program_id(1)
    n_sub = ts // SUBTILE
    chunks = SUBTILE // CHUNK
    piece_cols = [slice(lo, min(lo + PROJ_PIECE, IN_WIDTH)) for lo in range(0, IN_WIDTH, PROJ_PIECE)]
    n_pieces = len(piece_cols)

    @pl.when((pl.program_id(0) == 0) & (s_idx == 0))
    def _():
        def store_in(rows, w):
            win_ref[rows, 0:POOL_COL] = w[:, POOL_WIDTH:]
            win_ref[rows, POOL_COL:IN_WIDTH] = w[:, :POOL_WIDTH]

        def store_out(rows, w):
            wout_ref[rows, :] = w

        _load_weight_bf16(win_hbm.at[layer], win_stage, sems[0], store_in)
        _load_weight_bf16(wout_hbm.at[layer], wout_stage, sems[1], store_out)

    @pl.when(s_idx == 0)
    def _():
        st_ref[...] = jnp.zeros(st_ref.shape, F32)
        pbuf_ref[0:MAX_WIN, :] = jnp.zeros((MAX_WIN, POOL_WIDTH), F32)

    def norm_rows(sub, hn_ref, may_cross):
        rows = pl.ds(pl.multiple_of(jnp.minimum(sub, n_sub - 1) * SUBTILE, SUBTILE), SUBTILE)
        xs = x_ref[rows, :]
        if may_cross:
            xs = jnp.where(sub >= n_sub, xn_ref[...], xs)
        hn_ref[...] = _rms(xs, g_ref[...]).astype(BF16)

    def in_proj_piece(hn_ref, u_ref, k):
        cols = piece_cols[k]
        u_ref[:, cols] = jnp.dot(hn_ref[...], win_ref[:, cols], preferred_element_type=F32)

    def col(base, h):
        return slice(base + h * HEAD_DIM, base + (h + 1) * HEAD_DIM)


    chunk_rows = [slice(c * CHUNK, (c + 1) * CHUNK) for c in range(chunks)]
    items = [(c, h) for c in range(chunks) for h in range(HEADS)]

    def step(sub, half, st, filler):
        u, u_next = u_refs[half], u_refs[1 - half]
        hn_ref, y_ref, lk_ref = hn_refs[half], y_refs[half], lk_refs[half]
        filler = list(filler)
        per_item = -(-len(filler) // len(items))
        norm_rows(sub + 1, hn_ref, may_cross=(half == 1))
        n_slots = len(items) + 1

        def emit_pieces(slot):
            for k in range(n_pieces):
                if k * n_slots // n_pieces == slot:
                    in_proj_piece(hn_ref, u_next, k)

        def pool_mixer():
            pbuf_ref[MAX_WIN:MAX_WIN + SUBTILE, :] = u[:, POOL_COL:POOL_COL + POOL_WIDTH]
            feats = _pool_features(pbuf_ref, s_idx * ts + sub * SUBTILE).astype(BF16)
            y_pool = jnp.dot(feats, pw_ref[...], preferred_element_type=F32) * psc_ref[...]
            y_ref[:, 0:POOL_WIDTH] = y_pool.astype(BF16)
            pbuf_ref[0:MAX_WIN, :] = pbuf_ref[SUBTILE:SUBTILE + MAX_WIN, :]

        emit_pieces(0)

        stage1 = []
        for slot, (c, h) in enumerate(items, start=1):
            if slot == POOL_SLOT:
                pool_mixer()
            rows = chunk_rows[c]
            emit_pieces(slot)
            for thunk in filler[:per_item]:
                thunk()
            del filler[:per_item]
            a, st[h] = _hgrn_scores(
                u[rows, col(Q_COL, h)], u[rows, col(F_COL, h)], u[rows, col(I_COL, h)],
                lbp_ref[h, 0:1, :], lbp_ref[h, 1:2, :], lbp_ref[h, 2:3, :],
                st[h], mask_d_ref[...], lk_ref.at[c, h])
            a["gate"] = u[rows, col(G_COL, h)]
            stage1.append(a)

        stage2 = {}

        def mix(i):
            stage2[i] = _hgrn_mix(stage1[i], mask_p_ref[...])

        def finish(i):
            c, h = items[i]
            y = _hgrn_out(stage2.pop(i), stage1[i]["gate"], ng_ref[h])
            y_ref[chunk_rows[c], col(POOL_WIDTH, h)] = y.astype(BF16)

        def out_proj():
            rows = pl.ds(pl.multiple_of(sub * SUBTILE, SUBTILE), SUBTILE)
            o_ref[rows, :] = x_ref[rows, :] + jnp.dot(y_ref[...], wout_ref[...],
                                                      preferred_element_type=F32)

        tail = []
        for i in range(len(items) + STAGE_LAG):
            if i < len(items):
                tail.append(functools.partial(mix, i))
            if i >= STAGE_LAG:
                tail.append(functools.partial(finish, i - STAGE_LAG))
        tail.append(out_proj)
        return tail

    @pl.when(s_idx == 0)
    def _():
        norm_rows(0, hn_refs[1], may_cross=False)
        for k in range(n_pieces):
            in_proj_piece(hn_refs[1], u_refs[0], k)

    def body(pair, carry):
        st = [st_ref[h] for h in range(HEADS)]
        tail = step(2 * pair, 0, st, [])
        tail = step(2 * pair + 1, 1, st, tail)
        for thunk in tail:
            thunk()
        for h in range(HEADS):
            st_ref[h] = st[h]
        return carry

    lax.fori_loop(0, n_sub // 2, body, 0)


def _mixer(x3, g, w_in, lb_params, norm_g, pool_w_bd, pool_scale, w_out, ts, layer):
    b, s, d = x3.shape
    mask_p, mask_d = _score_masks()
    n_sub = ts // SUBTILE
    last_sub = s // SUBTILE - 1
    const2 = lambda shape: pl.BlockSpec(shape, lambda bi, si: (0, 0))
    const3 = lambda shape: pl.BlockSpec(shape, lambda bi, si: (0, 0, 0))
    return pl.pallas_call(
        functools.partial(_mixer_kernel, ts=ts, layer=layer),
        grid=(b, s // ts),
        in_specs=[
            pl.BlockSpec((None, ts, d), lambda bi, si: (bi, si, 0)),
            pl.BlockSpec((None, SUBTILE, d),
                         lambda bi, si: (bi, jnp.minimum((si + 1) * n_sub, last_sub), 0)),
            const2((1, d)),
            pl.BlockSpec(memory_space=pl.ANY),
            const3((HEADS, 8, HEAD_DIM)),
            const3((HEADS, 1, HEAD_DIM)),
            const2((CHUNK, STACK_PAD)),
            const2((CHUNK, LANES)),
            const2((POOL_WIDTH, POOL_WIDTH)),
            const2((1, POOL_WIDTH)),
            pl.BlockSpec(memory_space=pl.ANY),
        ],
        out_specs=pl.BlockSpec((None, ts, d), lambda bi, si: (bi, si, 0)),
        out_shape=jax.ShapeDtypeStruct((b, s, d), F32),
        scratch_shapes=[
            pltpu.VMEM((d, IN_WIDTH), BF16),
            pltpu.VMEM((d, d), BF16),
            pltpu.VMEM((2, WEIGHT_STAGE_ROWS, IN_WIDTH), F32),
            pltpu.VMEM((2, WEIGHT_STAGE_ROWS, d), F32),
            [pltpu.SemaphoreType.DMA((2,))] * 2,
            [pltpu.VMEM((SUBTILE, IN_WIDTH), F32)] * 2,
            [pltpu.VMEM((SUBTILE, d), BF16)] * 2,
            [pltpu.VMEM((SUBTILE, d), BF16)] * 2,
            [pltpu.VMEM((SUBTILE // CHUNK, HEADS, CHUNK, LANES), F32)] * 2,
            pltpu.VMEM((HEADS, HEAD_DIM, HEAD_DIM), F32),
            pltpu.VMEM((SUBTILE + MAX_WIN, POOL_WIDTH), F32),
        ],
        compiler_params=pltpu.CompilerParams(
            dimension_semantics=("arbitrary", "arbitrary"), vmem_limit_bytes=VMEM_LIMIT),
        name="mixer",
    )(x3, x3, g, w_in, lb_params, norm_g, mask_p, mask_d, pool_w_bd, pool_scale, w_out)


def _mlp_kernel(x_ref, g_ref, wu_hbm, wd_hbm, fg_ref, o_ref,
                wu_ref, wd_ref, wu_stage, wd_stage, sems, *, ff_tile, final_norm, layer):
    @pl.when(pl.program_id(0) == 0)
    def _():
        def store_up(rows, w):
            wu_ref[rows, :] = w

        def store_down(rows, w):
            wd_ref[rows, :] = w

        _load_weight_bf16(wu_hbm.at[layer], wu_stage, sems[0], store_up)
        _load_weight_bf16(wd_hbm.at[layer], wd_stage, sems[1], store_down)

    x = x_ref[...]
    h = _rms(x, g_ref[...]).astype(BF16)
    acc = x
    for n in range(D_FF // ff_tile):
        cols = slice(n * ff_tile, (n + 1) * ff_tile)
        a = jnp.dot(h, wu_ref[:, cols], preferred_element_type=F32)
        a = jnp.square(jnp.maximum(a, 0.0)).astype(BF16)
        acc = acc + jnp.dot(a, wd_ref[cols, :], preferred_element_type=F32)
    o_ref[...] = _rms(acc, fg_ref[...]) if final_norm else acc


def _mlp(x2, g, w_up, w_down, final_g, tm, layer, final_norm, ff_tile=1024):
    t = x2.shape[0]
    return pl.pallas_call(
        functools.partial(_mlp_kernel, ff_tile=ff_tile, final_norm=final_norm, layer=layer),
        grid=(t // tm,),
        in_specs=[
            pl.BlockSpec((tm, D_MODEL), lambda i: (i, 0)),
            pl.BlockSpec((1, D_MODEL), lambda i: (0, 0)),
            pl.BlockSpec(memory_space=pl.ANY),
            pl.BlockSpec(memory_space=pl.ANY),
            pl.BlockSpec((1, D_MODEL), lambda i: (0, 0)),
        ],
        out_specs=pl.BlockSpec((tm, D_MODEL), lambda i: (i, 0)),
        out_shape=jax.ShapeDtypeStruct((t, D_MODEL), F32),
        scratch_shapes=[
            pltpu.VMEM((D_MODEL, D_FF), BF16),
            pltpu.VMEM((D_FF, D_MODEL), BF16),
            pltpu.VMEM((2, WEIGHT_STAGE_ROWS, D_FF), F32),
            pltpu.VMEM((2, 4 * WEIGHT_STAGE_ROWS, D_MODEL), F32),
            [pltpu.SemaphoreType.DMA((2,))] * 2,
        ],
        compiler_params=pltpu.CompilerParams(
            dimension_semantics=("arbitrary",), vmem_limit_bytes=VMEM_LIMIT),
        name="mlp",
    )(x2, g, w_up, w_down, final_g)


def _block_diag(w):
    g, c, _ = w.shape
    eye = jnp.eye(g, dtype=w.dtype)
    return (eye[:, None, :, None] * w[:, :, None, :]).reshape(g * c, g * c)


def kernel(x, norm_mix_g, w_in, pool_w, pool_scale, hgrn_lb_logits, hgrn_norm_g, w_out,
           norm_mlp_g, w_up, w_down, final_norm_g):
    b, s, d = x.shape
    depth = w_in.shape[0]
    t = b * s
    tm = 1024
    ts = 2048

    lb_cum = jnp.cumsum(jax.nn.softmax(hgrn_lb_logits.astype(F32), axis=0), axis=0)
    lower = lb_cum - lb_cum[0:1]
    lbp = jnp.stack([lower, 1.0 - lower, jnp.log1p(-lower)], axis=1)
    lbp = jnp.pad(lbp, ((0, 0), (0, 5), (0, 0)))
    lbp = lbp.reshape(depth, 8, HEADS, HEAD_DIM).transpose(0, 2, 1, 3)

    for l in range(depth):
        x = _mixer(x, norm_mix_g[l][None, :], w_in, lbp[l],
                   hgrn_norm_g[l].reshape(HEADS, 1, HEAD_DIM),
                   _block_diag(pool_w[l]).astype(BF16), pool_scale[l][None, :],
                   w_out, ts, layer=l)
        x = _mlp(x.reshape(t, d), norm_mlp_g[l][None, :], w_up, w_down,
                 final_norm_g[None, :], tm, layer=l,
                 final_norm=(l == depth - 1)).reshape(b, s, d)
    return x
```
